```python
import jax, jax.numpy as jnp
from jax import lax
import numpy as np

D_MODEL = 2048
BATCH = 2
SEQ = 8192
DEPTH = 1

GRID_W = 64
NA_HEADS = 8
NA_HEAD_DIM = 128
NA_WIDTH = NA_HEADS * NA_HEAD_DIM
NA_KH = 8
NA_KW = 16
NA_COLS = 3 * NA_WIDTH
RWKV_HEAD_DIM = 64
RWKV_HEADS = 16
RWKV_WIDTH = RWKV_HEADS * RWKV_HEAD_DIM
DECAY_LORA = 96
AAA_LORA = 96
GATE_LORA = 256
RWKV_GN_EPS = 64e-5
RWKV_SIZES = (RWKV_WIDTH, RWKV_WIDTH, RWKV_WIDTH, DECAY_LORA, DECAY_LORA, AAA_LORA, AAA_LORA, GATE_LORA)
RWKV_COLS = 3 * RWKV_WIDTH + 2 * DECAY_LORA + 2 * AAA_LORA + GATE_LORA
GATE_COLS = 2 * D_MODEL
IN_COLS = NA_COLS + RWKV_COLS + GATE_COLS
PEER_HEADS = 8
PEER_DQ = 256
PEER_N_KEYS = 128
PEER_N_EXPERTS = PEER_N_KEYS * PEER_N_KEYS
PEER_TOPK = 16
PEER_CHUNK = 128
NORM_EPS = 1e-6

kernel_name = "hybrid_na_rwkv7_peer_block"


def _split(z, sizes):
    parts, start = [], 0
    for s in sizes:
        parts.append(z[..., start:start + s])
        start += s
    return parts


def rms_norm(x, w):
    xf = x.astype(jnp.float32)
    y = xf * lax.rsqrt(jnp.mean(xf * xf, axis=-1, keepdims=True) + NORM_EPS)
    return y.astype(x.dtype) * w


def neighbourhood_attention(q, k, v, rpb):
    B, S, H, Dh = q.shape
    rows = S // GRID_W
    kh = min(NA_KH, rows)
    q_g = q.reshape(B, rows, GRID_W, H, Dh)
    k_g = k.reshape(B, rows, GRID_W, H, Dh)
    v_g = v.reshape(B, rows, GRID_W, H, Dh)
    row_q = jnp.arange(rows)
    row_start = jnp.clip(row_q - kh // 2, 0, rows - kh)
    row_idx = row_start[:, None] + jnp.arange(kh)
    k_band = k_g[:, row_idx]
    v_band = v_g[:, row_idx]
    logits = jnp.einsum('biqhd,birkhd->bhiqrk', q_g, k_band).astype(jnp.float32) * (Dh ** -0.5)
    col = jnp.arange(GRID_W)
    col_start = jnp.clip(col - NA_KW // 2, 0, GRID_W - NA_KW)
    col_mask = (col[None, :] >= col_start[:, None]) & (col[None, :] < col_start[:, None] + NA_KW)
    dr = row_idx - row_q[:, None] + NA_KH - 1
    dc = jnp.clip(col[None, :] - col[:, None], -(NA_KW - 1), NA_KW - 1) + NA_KW - 1
    bias = rpb[:, dr[:, None, :, None], dc[None, :, None, :]]
    logits = jnp.where(col_mask[:, None, :], logits + bias.astype(jnp.float32)[None], -jnp.inf)
    probs = jax.nn.softmax(logits.reshape(B, H, rows, GRID_W, kh * GRID_W), axis=-1)
    probs = probs.reshape(B, H, rows, GRID_W, kh, GRID_W).astype(v.dtype)
    out = jnp.einsum('bhiqrk,birkhd->biqhd', probs, v_band)
    return out.reshape(B, S, H * Dh)


def centred_token_shift(y, mu):
    prev = jnp.pad(y[:, :-1], ((0, 0), (1, 0), (0, 0)))
    nxt = jnp.pad(y[:, 1:], ((0, 0), (0, 1), (0, 0)))
    return y + mu[0] * (prev - y) + mu[1] * (nxt - y)


def rwkv7_bidirectional(zb, mu, w0, w2, a0, a2, g2, k_k, k_a, r_k, ln_w, ln_b):
    B, S, _ = zb.shape
    H, N, C = RWKV_HEADS, RWKV_HEAD_DIM, RWKV_WIDTH
    f32 = jnp.float32
    zb = centred_token_shift(zb, mu)
    r, k, v, wd_f, wd_b, ad_f, ad_b, gd = _split(zb, RWKV_SIZES)
    wd = jnp.stack([wd_f, wd_b])
    ad = jnp.stack([ad_f, ad_b])
    w_logit = w0[:, None, None, :] + jnp.einsum('nbsr,nrc->nbsc', jnp.tanh(wd), w2)
    decay = jnp.exp(-jnp.exp(-jax.nn.softplus(-w_logit.astype(f32)) - 0.5))
    iclr = jax.nn.sigmoid((a0[:, None, None, :] + jnp.einsum('nbsr,nrc->nbsc', ad, a2)).astype(f32))
    g = (jax.nn.sigmoid(gd) @ g2).astype(f32)
    rf, kf, vf = r.astype(f32), k.astype(f32), v.astype(f32)
    kk = (kf * k_k.astype(f32)).reshape(B, S, H, N)
    kk = kk / jnp.maximum(jnp.sqrt(jnp.sum(kk * kk, axis=-1, keepdims=True)), 1e-12)
    kk = kk.reshape(B, S, C)
    k_dir = kf[None] * (1.0 + (iclr - 1.0) * k_a.astype(f32))
    b_dir = kk[None] * iclr

    def directional(t):
        t = jnp.stack([t[0], jnp.flip(t[1], axis=1)])
        return jnp.moveaxis(t.reshape(2, B, S, H, N), 2, 0)

    def shared(t):
        return directional(jnp.stack([t, t]))

    xs = (shared(rf), directional(decay), directional(k_dir), shared(vf), shared(-kk), directional(b_dir))

    def step(state, inp):
        r_t, w_t, k_t, v_t, negkk_t, b_t = inp
        sa = jnp.einsum('nbhvk,nbhk->nbhv', state, negkk_t)
        state = state * w_t[..., None, :] + sa[..., :, None] * b_t[..., None, :] + v_t[..., :, None] * k_t[..., None, :]
        return state, jnp.einsum('nbhvk,nbhk->nbhv', state, r_t)

    state0 = jnp.zeros((2, B, H, N, N), f32)
    _, o = lax.scan(step, state0, xs)
    o = jnp.moveaxis(o[:, 0] + jnp.flip(o[:, 1], axis=0), 0, 1)
    mean = jnp.mean(o, axis=-1, keepdims=True)
    var = jnp.mean(jnp.square(o - mean), axis=-1, keepdims=True)
    o = ((o - mean) * lax.rsqrt(var + RWKV_GN_EPS)).reshape(B, S, C) * ln_w.astype(f32) + ln_b.astype(f32)
    bonus = jnp.sum(rf.reshape(B, S, H, N) * kf.reshape(B, S, H, N) * r_k.astype(f32), axis=-1, keepdims=True) * vf.reshape(B, S, H, N)
    y = (o + bonus.reshape(B, S, C)) * g
    return y.astype(zb.dtype)


def peer_ffn(u, w_q, sub_k1, sub_k2, exp_u, exp_v):
    B, S, D = u.shape
    T = B * S
    half = PEER_DQ // 2
    q = (u @ w_q).reshape(T, PEER_HEADS, PEER_DQ)
    s1 = jnp.einsum('thd,nd->thn', q[..., :half], sub_k1).astype(jnp.float32)
    s2 = jnp.einsum('thd,nd->thn', q[..., half:], sub_k2).astype(jnp.float32)
    v1, i1 = lax.top_k(s1, PEER_TOPK)
    v2, i2 = lax.top_k(s2, PEER_TOPK)
    cand_s = (v1[..., :, None] + v2[..., None, :]).reshape(T, PEER_HEADS, PEER_TOPK * PEER_TOPK)
    cand_i = (i1[..., :, None] * PEER_N_KEYS + i2[..., None, :]).reshape(T, PEER_HEADS, PEER_TOPK * PEER_TOPK)
    top_s, pos = lax.top_k(cand_s, PEER_TOPK)
    expert_idx = jnp.take_along_axis(cand_i, pos, axis=-1)
    gate = jax.nn.softmax(top_s, axis=-1).astype(u.dtype)
    n_chunks = T // PEER_CHUNK

    def chunk(args):
        xc, ec, gc = args
        hidden = jax.nn.gelu(jnp.einsum('cd,chkd->chk', xc, exp_u[ec]), approximate=False) * gc
        return jnp.einsum('chk,chkd->cd', hidden, exp_v[ec])

    out = lax.map(chunk, (u.reshape(n_chunks, PEER_CHUNK, D),
                          expert_idx.reshape(n_chunks, PEER_CHUNK, PEER_HEADS, PEER_TOPK),
                          gate.reshape(n_chunks, PEER_CHUNK, PEER_HEADS, PEER_TOPK)))
    return out.reshape(B, S, D)


def hybrid_block(x, c, ada_w, ada_b, norm1_w, norm2_w, w_in, b_gate, na_q_norm_w, na_k_norm_w, na_rpb,
                 rwkv_mu, rwkv_w0, rwkv_w2, rwkv_a0, rwkv_a2, rwkv_g2, rwkv_k_k, rwkv_k_a, rwkv_r_k,
                 rwkv_ln_w, rwkv_ln_b, w_branch_a, w_branch_b, w_out, peer_w_q, peer_sub_k1, peer_sub_k2,
                 peer_u, peer_v):
    B, S, D = x.shape
    mod = jax.nn.silu(c) @ ada_w + ada_b
    shift1, scale1, gate1, shift2, scale2, gate2 = jnp.split(mod[:, None, :], 6, axis=-1)
    u = rms_norm(x, norm1_w) * (1.0 + scale1) + shift1
    z = u @ w_in
    za, zb, zg = _split(z, (NA_COLS, RWKV_COLS, GATE_COLS))
    q, k, v = jnp.split(za, 3, axis=-1)
    q = rms_norm(q.reshape(B, S, NA_HEADS, NA_HEAD_DIM), na_q_norm_w)
    k = rms_norm(k.reshape(B, S, NA_HEADS, NA_HEAD_DIM), na_k_norm_w)
    v = v.reshape(B, S, NA_HEADS, NA_HEAD_DIM)
    o_a = neighbourhood_attention(q, k, v, na_rpb)
    o_b = rwkv7_bidirectional(zb, rwkv_mu, rwkv_w0, rwkv_w2, rwkv_a0, rwkv_a2, rwkv_g2,
                              rwkv_k_k, rwkv_k_a, rwkv_r_k, rwkv_ln_w, rwkv_ln_b)
    gates = jax.nn.sigmoid((zg + b_gate).astype(jnp.float32)).astype(x.dtype)
    gate_a, gate_b = jnp.split(gates, 2, axis=-1)
    merged = gate_a * (o_a @ w_branch_a) + gate_b * (o_b @ w_branch_b)
    x = x + gate1 * (merged @ w_out)
    u2 = rms_norm(x, norm2_w) * (1.0 + scale2) + shift2
    x = x + gate2 * peer_ffn(u2, peer_w_q, peer_sub_k1, peer_sub_k2, peer_u, peer_v)
    return x


def setup_inputs(seed: int = 0) -> dict:
    key = jax.random.key(seed)
    ks = iter(jax.random.split(key, 40))
    L = DEPTH

    def nrm(shape, scale):
        return scale * jax.random.normal(next(ks), shape, jnp.float32)

    def unif(shape, lo, hi):
        return jax.random.uniform(next(ks), shape, jnp.float32, minval=lo, maxval=hi)

    return {
        "x": nrm((BATCH, SEQ, D_MODEL), 1.0),
        "c": nrm((BATCH, D_MODEL), 1.0),
        "ada_w": nrm((L, D_MODEL, 6 * D_MODEL), 0.5 * D_MODEL ** -0.5),
        "ada_b": nrm((L, 6 * D_MODEL), 0.01),
        "norm1_w": 1.0 + nrm((L, D_MODEL), 0.02),
        "norm2_w": 1.0 + nrm((L, D_MODEL), 0.02),
        "w_in": nrm((L, D_MODEL, IN_COLS), D_MODEL ** -0.5),
        "b_gate": nrm((L, GATE_COLS), 0.1),
        "na_q_norm_w": 1.0 + nrm((L, NA_HEAD_DIM), 0.02),
        "na_k_norm_w": 1.0 + nrm((L, NA_HEAD_DIM), 0.02),
        "na_rpb": nrm((L, NA_HEADS, 2 * NA_KH - 1, 2 * NA_KW - 1), 0.1),
        "rwkv_mu": unif((L, 2, RWKV_COLS), 0.0, 0.5),
        "rwkv_w0": unif((L, 2, RWKV_WIDTH), -3.0, 1.0),
        "rwkv_w2": nrm((L, 2, DECAY_LORA, RWKV_WIDTH), DECAY_LORA ** -0.5),
        "rwkv_a0": nrm((L, 2, RWKV_WIDTH), 0.5),
        "rwkv_a2": nrm((L, 2, AAA_LORA, RWKV_WIDTH), AAA_LORA ** -0.5),
        "rwkv_g2": nrm((L, GATE_LORA, RWKV_WIDTH), GATE_LORA ** -0.5),
        "rwkv_k_k": 0.85 + nrm((L, RWKV_WIDTH), 0.02),
        "rwkv_k_a": 1.0 + nrm((L, RWKV_WIDTH), 0.02),
        "rwkv_r_k": nrm((L, RWKV_HEADS, RWKV_HEAD_DIM), 0.1),
        "rwkv_ln_w": 1.0 + nrm((L, RWKV_WIDTH), 0.02),
        "rwkv_ln_b": nrm((L, RWKV_WIDTH), 0.01),
        "w_branch_a": nrm((L, NA_WIDTH, D_MODEL), NA_WIDTH ** -0.5),
        "w_branch_b": nrm((L, RWKV_WIDTH, D_MODEL), RWKV_WIDTH ** -0.5),
        "w_out": nrm((L, D_MODEL, D_MODEL), D_MODEL ** -0.5),
        "peer_w_q": nrm((L, D_MODEL, PEER_HEADS * PEER_DQ), D_MODEL ** -0.5),
        "peer_sub_k1": nrm((L, PEER_N_KEYS, PEER_DQ // 2), (PEER_DQ // 2) ** -0.5),
        "peer_sub_k2": nrm((L, PEER_N_KEYS, PEER_DQ // 2), (PEER_DQ // 2) ** -0.5),
        "peer_u": nrm((L, PEER_N_EXPERTS, D_MODEL), D_MODEL ** -0.5),
        "peer_v": nrm((L, PEER_N_EXPERTS, D_MODEL), 0.5),
    }


def reference(x, c, ada_w, ada_b, norm1_w, norm2_w, w_in, b_gate, na_q_norm_w, na_k_norm_w, na_rpb,
              rwkv_mu, rwkv_w0, rwkv_w2, rwkv_a0, rwkv_a2, rwkv_g2, rwkv_k_k, rwkv_k_a, rwkv_r_k,
              rwkv_ln_w, rwkv_ln_b, w_branch_a, w_branch_b, w_out, peer_w_q, peer_sub_k1, peer_sub_k2,
              peer_u, peer_v):
    h = x
    for layer in range(DEPTH):
        h = hybrid_block(h, c, ada_w[layer], ada_b[layer], norm1_w[layer], norm2_w[layer], w_in[layer],
                         b_gate[layer], na_q_norm_w[layer], na_k_norm_w[layer], na_rpb[layer],
                         rwkv_mu[layer], rwkv_w0[layer], rwkv_w2[layer], rwkv_a0[layer], rwkv_a2[layer],
                         rwkv_g2[layer], rwkv_k_k[layer], rwkv_k_a[layer], rwkv_r_k[layer],
                         rwkv_ln_w[layer], rwkv_ln_b[layer], w_branch_a[layer], w_branch_b[layer],
                         w_out[layer], peer_w_q[layer], peer_sub_k1[layer], peer_sub_k2[layer],
                         peer_u[layer], peer_v[layer])
    return h
```

```python
import functools

import jax
import jax.numpy as jnp
from jax import lax
from jax.experimental import pallas as pl
from jax.experimental.pallas import tpu as pltpu

F32 = jnp.float32
BF16 = jnp.bfloat16

GRID_W = 64
NA_HEADS = 8
NA_HEAD_DIM = 128
NA_WIDTH = NA_HEADS * NA_HEAD_DIM
NA_KH = 8
NA_KW = 16
RWKV_HEAD_DIM = 64
RWKV_HEADS = 16
RWKV_WIDTH = RWKV_HEADS * RWKV_HEAD_DIM
DECAY_LORA = 96
AAA_LORA = 96
GATE_LORA = 256
RWKV_GN_EPS = 64e-5
PEER_HEADS = 8
PEER_DQ = 256
PEER_N_KEYS = 128
PEER_TOPK = 16
NORM_EPS = 1e-6

LANE = 128
VMEM_LIMIT = 56 * 1024 * 1024
NEG = -1e30

COL_QKV = 0
COL_RKV = 3 * NA_WIDTH
COL_GATE = COL_RKV + 3 * RWKV_WIDTH
COL_LORA = COL_GATE + 4096
IN_PAD = COL_LORA + 1024
LORA_W = 1024

RW_CHUNK = 64


def _cparams(sem):
    return pltpu.CompilerParams(dimension_semantics=sem, vmem_limit_bytes=VMEM_LIMIT)


def _dot(a, b):
    return jnp.dot(a, b, preferred_element_type=F32)


def _dot_nt(a, b):
    return lax.dot_general(a, b, (((1,), (1,)), ((), ())), preferred_element_type=F32)


def _dot_tn(a, b):
    return lax.dot_general(a, b, (((0,), (0,)), ((), ())), preferred_element_type=F32)


def _split_dot(a, b):
    hi = a.astype(BF16)
    lo = (a - hi.astype(F32)).astype(BF16)
    return _dot(hi, b) + _dot(lo, b)


def _adaln_kernel(c_ref, w_ref, b_ref, o_ref):
    c = c_ref[...]
    s = c * jax.nn.sigmoid(c)
    o_ref[...] = jnp.dot(s, w_ref[...], preferred_element_type=F32,
                         precision=lax.Precision.HIGHEST) + b_ref[...]


def _adaln(c, ada_w, ada_b):
    B, D = c.shape
    n = ada_w.shape[1]
    tn = 1024
    cp = jnp.zeros((8, D), F32).at[:B].set(c)
    out = pl.pallas_call(
        _adaln_kernel,
        out_shape=jax.ShapeDtypeStruct((8, n), F32),
        grid=(n // tn,),
        in_specs=[pl.BlockSpec((8, D), lambda j: (0, 0)),
                  pl.BlockSpec((D, tn), lambda j: (0, j)),
                  pl.BlockSpec((1, tn), lambda j: (0, j))],
        out_specs=pl.BlockSpec((8, tn), lambda j: (0, j)),
        compiler_params=_cparams(("arbitrary",)),
        name="adaln",
    )(cp, ada_w, ada_b.reshape(1, n))
    return out[:B].reshape(B, 6, D)


def _inproj_kernel(x_ref, mod_ref, nw_ref, w_ref, o_ref, u_ref):
    @pl.when(pl.program_id(1) == 0)
    def _():
        x = x_ref[...]
        ms = jnp.mean(x * x, axis=-1, keepdims=True)
        y = x * lax.rsqrt(ms + NORM_EPS) * nw_ref[...]
        u = y * (1.0 + mod_ref[0, 1:2, :]) + mod_ref[0, 0:1, :]
        u_ref[...] = u.astype(BF16)

    o_ref[...] = _dot(u_ref[...], w_ref[...])


def _inproj(x2, mod, norm_w, w_pack, S):
    T, D = x2.shape
    n = w_pack.shape[1]
    tm = min(1024, S)
    tn = 1024
    tpb = S // tm
    return pl.pallas_call(
        _inproj_kernel,
        out_shape=jax.ShapeDtypeStruct((T, n), F32),
        grid=(T // tm, n // tn),
        in_specs=[pl.BlockSpec((tm, D), lambda i, j: (i, 0)),
                  pl.BlockSpec((1, 6, D), lambda i, j: (i // tpb, 0, 0)),
                  pl.BlockSpec((1, D), lambda i, j: (0, 0)),
                  pl.BlockSpec((D, tn), lambda i, j: (0, j))],
        out_specs=pl.BlockSpec((tm, tn), lambda i, j: (i, j)),
        scratch_shapes=[pltpu.VMEM((tm, D), BF16)],
        compiler_params=_cparams(("arbitrary", "arbitrary")),
        name="inproj",
    )(x2, mod, norm_w.reshape(1, D), w_pack)


NA_QROWS = 8
NA_BAND = 16
NA_TQ = NA_QROWS * GRID_W
NA_KBLK = 4 * GRID_W


def _na_bias_tables(rpb, rows):
    kh = NA_KH
    col = jnp.arange(GRID_W)
    col_start = jnp.clip(col - NA_KW // 2, 0, GRID_W - NA_KW)
    col_mask = (col[None, :] >= col_start[:, None]) & (col[None, :] < col_start[:, None] + NA_KW)
    dc = jnp.clip(col[None, :] - col[:, None], -(NA_KW - 1), NA_KW - 1) + NA_KW - 1
    tables = []
    for i0 in (0, min(NA_QROWS, rows - NA_QROWS), rows - NA_QROWS):
        b0 = min(max(i0 - 4, 0), rows - NA_BAND)
        i = i0 + jnp.arange(NA_QROWS)
        kr = b0 + jnp.arange(NA_BAND)
        rs = jnp.clip(i - kh // 2, 0, rows - kh)
        valid_r = (kr[None, :] >= rs[:, None]) & (kr[None, :] < rs[:, None] + kh)
        dr = jnp.clip(kr[None, :] - i[:, None] + NA_KH - 1, 0, 2 * NA_KH - 2)
        bias = rpb[:, dr[:, None, :, None], dc[None, :, None, :]]
        ok = valid_r[:, None, :, None] & col_mask[None, :, None, :]
        bias = jnp.where(ok[None], bias, NEG)
        tables.append(bias.reshape(rpb.shape[0], NA_TQ, NA_BAND * GRID_W))
    return jnp.stack(tables)


def _na_kernel(q_ref, k0, k1, k2, k3, v0, v1, v2, v3, bias_ref, qw_ref, kw_ref, o_ref):
    q = q_ref[...]
    qn = q * lax.rsqrt(jnp.mean(q * q, axis=-1, keepdims=True) + NORM_EPS) * qw_ref[...]
    qn = (qn * (NA_HEAD_DIM ** -0.5)).astype(BF16)
    logits = []
    for n, kr in enumerate((k0, k1, k2, k3)):
        k = kr[...]
        kn = (k * lax.rsqrt(jnp.mean(k * k, axis=-1, keepdims=True) + NORM_EPS) * kw_ref[...]).astype(BF16)
        logits.append(_dot_nt(qn, kn) + bias_ref[0, 0, :, n * NA_KBLK:(n + 1) * NA_KBLK])
    m = logits[0].max(axis=-1, keepdims=True)
    for l in logits[1:]:
        m = jnp.maximum(m, l.max(axis=-1, keepdims=True))
    den = jnp.zeros_like(m)
    acc = jnp.zeros(o_ref.shape, F32)
    for l, vr in zip(logits, (v0, v1, v2, v3)):
        p = jnp.exp(l - m)
        den = den + p.sum(axis=-1, keepdims=True)
        acc = acc + _dot(p.astype(BF16), vr[...].astype(BF16))
    o_ref[...] = acc / den


def _na(z, bias, qw, kw, B, S):
    T = z.shape[0]
    rows = S // GRID_W
    nblk = rows // NA_QROWS
    nkb = S // NA_KBLK

    def kv_map(col0, n):
        def f(b, h, i):
            start = jnp.clip(2 * i - 1, 0, nkb - 4)
            return (b * nkb + start + n, col0 + h)
        return f

    def bias_map(b, h, i):
        ty = jnp.where(i == 0, 0, jnp.where(i == nblk - 1, 2, 1))
        return (ty, h, 0, 0)

    q_spec = pl.BlockSpec((NA_TQ, LANE), lambda b, h, i: (b * nblk + i, h))
    k_specs = [pl.BlockSpec((NA_KBLK, LANE), kv_map(NA_HEADS, n)) for n in range(4)]
    v_specs = [pl.BlockSpec((NA_KBLK, LANE), kv_map(2 * NA_HEADS, n)) for n in range(4)]
    w_spec = pl.BlockSpec((1, LANE), lambda b, h, i: (0, 0))
    return pl.pallas_call(
        _na_kernel,
        out_shape=jax.ShapeDtypeStruct((T, NA_WIDTH), F32),
        grid=(B, NA_HEADS, nblk),
        in_specs=[q_spec, *k_specs, *v_specs,
                  pl.BlockSpec((1, 1, NA_TQ, NA_BAND * GRID_W), bias_map), w_spec, w_spec],
        out_specs=pl.BlockSpec((NA_TQ, LANE), lambda b, h, i: (b * nblk + i, h)),
        compiler_params=_cparams(("arbitrary", "arbitrary", "arbitrary")),
        name="na_attn",
    )(z, z, z, z, z, z, z, z, z, bias, qw.reshape(1, LANE), kw.reshape(1, LANE))


def _rwkv_prep_kernel(r_ref, k_ref, v_ref, l_ref,
                      rp_ref, kp_ref, vp_ref, lp_ref,
                      rn_ref, kn_ref, vn_ref, ln_ref,
                      mu_ref, mul_ref, w0_ref, w2_ref, a0_ref, a2_ref, g2_ref, kk_ref, hs_ref,
                      ro_ref, ko_ref, vo_ref, kko_ref, iclr_ref, logw_ref, g_ref, *, tpb):
    i = pl.program_id(0)
    first = (i % tpb) == 0
    last = (i % tpb) == tpb - 1
    tm = r_ref.shape[0]
    rows = lax.broadcasted_iota(jnp.int32, (tm, 1), 0)

    def shift(y_ref, p_ref, n_ref, mu0, mu1):
        y = y_ref[...]
        prow = jnp.where(first, 0.0, p_ref[7:8, :])
        nrow = jnp.where(last, 0.0, n_ref[0:1, :])
        prev = jnp.where(rows == 0, prow, pltpu.roll(y, 1, 0))
        nxt = jnp.where(rows == tm - 1, nrow, pltpu.roll(y, tm - 1, 0))
        return y + mu0 * (prev - y) + mu1 * (nxt - y)

    r = shift(r_ref, rp_ref, rn_ref, mu_ref[0:1, 0:1024], mu_ref[1:2, 0:1024])
    k = shift(k_ref, kp_ref, kn_ref, mu_ref[0:1, 1024:2048], mu_ref[1:2, 1024:2048])
    v = shift(v_ref, vp_ref, vn_ref, mu_ref[0:1, 2048:3072], mu_ref[1:2, 2048:3072])
    lo = shift(l_ref, lp_ref, ln_ref, mul_ref[0:1, :], mul_ref[1:2, :])
    ro_ref[...] = r
    ko_ref[...] = k
    vo_ref[...] = v

    for d in range(2):
        wd = jnp.tanh(lo[:, d * LANE:(d + 1) * LANE]).astype(BF16)
        w_logit = w0_ref[d:d + 1, :] + _dot(wd, w2_ref[d])
        x = -w_logit
        softplus = jnp.maximum(x, 0.0) + jnp.log(1.0 + jnp.exp(-jnp.abs(x)))
        logw_ref[d] = -jnp.exp(-softplus - 0.5)
        ad = lo[:, (2 + d) * LANE:(3 + d) * LANE].astype(BF16)
        iclr_ref[d] = jax.nn.sigmoid(a0_ref[d:d + 1, :] + _dot(ad, a2_ref[d]))
    gd = jax.nn.sigmoid(lo[:, 4 * LANE:4 * LANE + GATE_LORA]).astype(BF16)
    g_ref[...] = _dot(gd, g2_ref[...])

    kk = k * kk_ref[...]
    ss = _split_dot(kk * kk, hs_ref[...])
    kko_ref[...] = kk / jnp.maximum(jnp.sqrt(ss), 1e-12)


def _rwkv_prep(z, mu_rkv, mu_lora, w0, w2p, a0, a2p, g2, k_k, headsum, S):
    T = z.shape[0]
    C = RWKV_WIDTH
    tm = 256
    tpb = S // tm
    nt = T // tm
    hb = tm // 8
    cb = COL_RKV // C
    lb = COL_LORA // LORA_W

    def main(col):
        return pl.BlockSpec((tm, C), lambda i: (i, col))

    def prev(col):
        return pl.BlockSpec((8, C), lambda i: (jnp.maximum(i * hb - 1, 0), col))

    def nxt(col):
        return pl.BlockSpec((8, C), lambda i: (jnp.minimum((i + 1) * hb, T // 8 - 1), col))

    def full(shape):
        nd = len(shape)
        return pl.BlockSpec(shape, lambda i: (0,) * nd)

    cols = (cb, cb + 1, cb + 2, lb)
    out_tc = jax.ShapeDtypeStruct((T, C), F32)
    out_2tc = jax.ShapeDtypeStruct((2, T, C), F32)
    spec_tc = pl.BlockSpec((tm, C), lambda i: (i, 0))
    spec_2tc = pl.BlockSpec((2, tm, C), lambda i: (0, i, 0))
    return pl.pallas_call(
        functools.partial(_rwkv_prep_kernel, tpb=tpb),
        out_shape=(out_tc, out_tc, out_tc, out_tc, out_2tc, out_2tc, out_tc),
        grid=(nt,),
        in_specs=[*[main(c) for c in cols], *[prev(c) for c in cols], *[nxt(c) for c in cols],
                  full(mu_rkv.shape), full(mu_lora.shape), full(w0.shape), full(w2p.shape),
                  full(a0.shape), full(a2p.shape), full(g2.shape), full((1, C)), full(headsum.shape)],
        out_specs=(spec_tc, spec_tc, spec_tc, spec_tc, spec_2tc, spec_2tc, spec_tc),
        compiler_params=_cparams(("arbitrary",)),
        name="rwkv_prep",
    )(z, z, z, z, z, z, z, z, z, z, z, z,
      mu_rkv, mu_lora, w0, w2p, a0, a2p, g2, k_k.reshape(1, C), headsum)


def _rwkv_scan_kernel(r_ref, k_ref, v_ref, kk_ref, iclr_ref, logw_ref, ka_ref, o_ref, s_ref):
    d = pl.program_id(0)
    L = RW_CHUNK
    n_pairs = RWKV_WIDTH // LANE

    @pl.when(pl.program_id(2) == 0)
    def _():
        s_ref[...] = jnp.zeros_like(s_ref)

    sgn = 1 - 2 * d
    ti = lax.broadcasted_iota(jnp.int32, (L, L), 0)
    si = lax.broadcasted_iota(jnp.int32, (L, L), 1)
    cum_mask = jnp.where((ti - si) * sgn >= 0, 1.0, 0.0).astype(BF16)
    R = lax.broadcasted_iota(jnp.int32, (2 * L, 2 * L), 0)
    Cc = lax.broadcasted_iota(jnp.int32, (2 * L, 2 * L), 1)
    same = (R >= L) == (Cc >= L)
    m_strict = same & ((R - Cc) * sgn > 0)
    m_incl = same & ((R - Cc) * sgn >= 0)
    eye = jnp.where(R == Cc, 1.0, 0.0)
    lane = lax.broadcasted_iota(jnp.int32, (1, LANE), 1)
    head0 = lane < RWKV_HEAD_DIM
    ones_l = jnp.ones((L, LANE), BF16)

    def stack(x):
        return jnp.concatenate([jnp.where(head0, x, 0.0), jnp.where(head0, 0.0, x)], axis=0)

    for p in range(n_pairs):
        sl = slice(p * LANE, (p + 1) * LANE)
        lw = logw_ref[0, :, sl]
        lw_hi = lw.astype(BF16)
        lw_lo = (lw - lw_hi.astype(F32)).astype(BF16)
        cs = _dot(cum_mask, lw_hi) + _dot(cum_mask, lw_lo)
        tot = _dot_tn(lw_hi, ones_l) + _dot_tn(lw_lo, ones_l)
        e_in = jnp.exp(cs)
        e_out = jnp.exp(-cs)
        kk = kk_ref[:, sl]
        iclr = iclr_ref[0, :, sl]
        kdir = k_ref[:, sl] * (1.0 + (iclr - 1.0) * ka_ref[:, sl])
        a_t = -kk * jnp.exp(cs - lw)
        b_t = kk * iclr * e_out
        k_t = kdir * e_out
        r_t = r_ref[:, sl] * e_in
        vs = stack(v_ref[:, sl]).astype(BF16)
        ar = jnp.concatenate([stack(a_t), stack(r_t)], axis=0).astype(BF16)
        bk = jnp.concatenate([stack(b_t), stack(k_t)], axis=0).astype(BF16)
        aa = _dot_nt(ar, bk)
        a_ab = jnp.where(m_strict, aa[:2 * L, :2 * L], 0.0)
        a_ak = jnp.where(m_strict, aa[:2 * L, 2 * L:], 0.0)
        a_rb = jnp.where(m_incl, aa[2 * L:, :2 * L], 0.0)
        a_rk = jnp.where(m_incl, aa[2 * L:, 2 * L:], 0.0)
        pw = a_ab
        tinv = eye + a_ab
        for _ in range(5):
            pwb = pw.astype(BF16)
            pw = _dot(pwb, pwb)
            tinv = tinv + _dot(tinv.astype(BF16), pw.astype(BF16))
        sp = s_ref[p]
        sa_sr = _dot(ar, sp.astype(BF16))
        x1 = sa_sr[:2 * L] + _dot(a_ak.astype(BF16), vs)
        u = _dot(tinv.astype(BF16), x1.astype(BF16))
        uv = jnp.concatenate([u.astype(BF16), vs], axis=0)
        o_s = sa_sr[2 * L:] + _dot(jnp.concatenate([a_rb, a_rk], axis=1).astype(BF16), uv)
        o_ref[0, :, sl] = o_s[:L] + o_s[L:]
        s_ref[p] = (sp + _dot_tn(bk, uv)) * jnp.exp(tot)


def _rwkv_scan(r, k, v, kk, iclr, logw, k_a, B, S):
    T, C = r.shape
    L = RW_CHUNK
    nc = S // L

    def row(dd, b, c):
        return b * nc + c + dd * (nc - 1 - 2 * c)

    tc = pl.BlockSpec((L, C), lambda dd, b, c: (row(dd, b, c), 0))
    dtc = pl.BlockSpec((1, L, C), lambda dd, b, c: (dd, row(dd, b, c), 0))
    return pl.pallas_call(
        _rwkv_scan_kernel,
        out_shape=jax.ShapeDtypeStruct((2, T, C), F32),
        grid=(2, B, nc),
        in_specs=[tc, tc, tc, tc, dtc, dtc, pl.BlockSpec((1, C), lambda dd, b, c: (0, 0))],
        out_specs=dtc,
        scratch_shapes=[pltpu.VMEM((C // LANE, LANE, LANE), F32)],
        compiler_params=_cparams(("arbitrary", "arbitrary", "arbitrary")),
        name="rwkv_scan",
    )(r, k, v, kk, iclr, logw, k_a.reshape(1, C))


def _merge_kernel(o_ref, r_ref, k_ref, v_ref, g_ref, oa_ref, za_ref, zb_ref,
                  hs_ref, rk_ref, lnw_ref, lnb_ref, bga_ref, bgb_ref, wa_ref, wb_ref,
                  out_ref, yb_ref, oab_ref):
    @pl.when(pl.program_id(1) == 0)
    def _():
        hs = hs_ref[...]
        o = o_ref[0] + o_ref[1]
        inv_n = 1.0 / RWKV_HEAD_DIM
        mean = _split_dot(o, hs) * inv_n
        cen = o - mean
        var = _split_dot(cen * cen, hs) * inv_n
        y = cen * lax.rsqrt(var + RWKV_GN_EPS) * lnw_ref[...] + lnb_ref[...]
        v = v_ref[...]
        bonus = _split_dot(r_ref[...] * k_ref[...] * rk_ref[...], hs) * v
        yb_ref[...] = ((y + bonus) * g_ref[...]).astype(BF16)
        oab_ref[...] = oa_ref[...].astype(BF16)

    ga = jax.nn.sigmoid(za_ref[...] + bga_ref[...])
    gb = jax.nn.sigmoid(zb_ref[...] + bgb_ref[...])
    out_ref[...] = ga * _dot(oab_ref[...], wa_ref[...]) + gb * _dot(yb_ref[...], wb_ref[...])


def _merge(o2, r, k, v, g, oa, z, headsum, r_k, ln_w, ln_b, b_gate, wa, wb, D):
    T, C = r.shape
    tm, tn = 256, 1024
    nj = D // tn
    gcol = COL_GATE // tn
    tc = pl.BlockSpec((tm, C), lambda i, j: (i, 0))

    def vec(n):
        return pl.BlockSpec((1, n), lambda i, j: (0, 0))

    return pl.pallas_call(
        _merge_kernel,
        out_shape=jax.ShapeDtypeStruct((T, D), F32),
        grid=(T // tm, nj),
        in_specs=[pl.BlockSpec((2, tm, C), lambda i, j: (0, i, 0)), tc, tc, tc, tc, tc,
                  pl.BlockSpec((tm, tn), lambda i, j: (i, gcol + j)),
                  pl.BlockSpec((tm, tn), lambda i, j: (i, gcol + nj + j)),
                  pl.BlockSpec((C, C), lambda i, j: (0, 0)), vec(C), vec(C), vec(C),
                  pl.BlockSpec((1, tn), lambda i, j: (0, j)),
                  pl.BlockSpec((1, tn), lambda i, j: (0, nj + j)),
                  pl.BlockSpec((C, tn), lambda i, j: (0, j)),
                  pl.BlockSpec((C, tn), lambda i, j: (0, j))],
        out_specs=pl.BlockSpec((tm, tn), lambda i, j: (i, j)),
        scratch_shapes=[pltpu.VMEM((tm, C), BF16), pltpu.VMEM((tm, C), BF16)],
        compiler_params=_cparams(("arbitrary", "arbitrary")),
        name="merge",
    )(o2, r, k, v, g, oa, z, z, headsum, r_k.reshape(1, C), ln_w.reshape(1, C), ln_b.reshape(1, C),
      b_gate.reshape(1, 2 * D), b_gate.reshape(1, 2 * D), wa, wb)


def _outproj_kernel(m_ref, x_ref, mod_ref, nw_ref, w_ref, x1_ref, u2_ref):
    x1 = x_ref[...] + mod_ref[0, 2:3, :] * _dot(m_ref[...].astype(BF16), w_ref[...])
    x1_ref[...] = x1
    ms = jnp.mean(x1 * x1, axis=-1, keepdims=True)
    y = x1 * lax.rsqrt(ms + NORM_EPS) * nw_ref[...]
    u2_ref[...] = (y * (1.0 + mod_ref[0, 4:5, :]) + mod_ref[0, 3:4, :]).astype(BF16)


def _outproj(merged, x2, mod, norm_w, w_out, S):
    T, D = x2.shape
    tm = 256
    tpb = S // tm
    td = pl.BlockSpec((tm, D), lambda i: (i, 0))
    return pl.pallas_call(
        _outproj_kernel,
        out_shape=(jax.ShapeDtypeStruct((T, D), F32), jax.ShapeDtypeStruct((T, D), BF16)),
        grid=(T // tm,),
        in_specs=[td, td, pl.BlockSpec((1, 6, D), lambda i: (i // tpb, 0, 0)),
                  pl.BlockSpec((1, D), lambda i: (0, 0)), pl.BlockSpec((D, D), lambda i: (0, 0))],
        out_specs=(td, td),
        compiler_params=_cparams(("arbitrary",)),
        name="outproj",
    )(merged, x2, mod, norm_w.reshape(1, D), w_out)


PEER_NV = PEER_TOPK + 1
PEER_VPAD = 24


def _peer_topk_kernel(u_ref, wq_ref, k1_ref, k2_ref, e1_ref, th_ref, e2_ref, vals_ref):
    tm = u_ref.shape[0]
    half = PEER_DQ // 2
    q = _dot(u_ref[...], wq_ref[...])
    vals_ref[...] = jnp.full(vals_ref.shape, NEG, F32)
    for h in range(PEER_HEADS):
        q1 = q[:, h * PEER_DQ:h * PEER_DQ + half].astype(BF16)
        q2 = q[:, h * PEER_DQ + half:(h + 1) * PEER_DQ].astype(BF16)
        s1 = _dot_nt(k1_ref[...], q1)
        s2 = _dot_nt(k2_ref[...], q2)
        for side, s in enumerate((s1, s2)):
            for it in range(PEER_NV):
                m = s.max(axis=0, keepdims=True)
                vals_ref[side, it:it + 1, :] = m
                s = jnp.where(s >= m, NEG, s)
        v1 = vals_ref[0]
        v2 = vals_ref[1]
        cands = [v1[a:a + 1, :] + v2[0:8, :] for a in range(8)]
        cands.append(v1[0:1, :] + v2[8:PEER_VPAD, :])
        cands.append(v1[8:PEER_VPAD, :] + v2[0:1, :])
        cand = jnp.concatenate(cands, axis=0)
        crow = lax.broadcasted_iota(jnp.int32, cand.shape, 0)
        tops = []
        for it in range(PEER_NV):
            m = cand.max(axis=0, keepdims=True)
            tops.append(m)
            first = jnp.where(cand >= m, crow, cand.shape[0]).min(axis=0, keepdims=True)
            cand = jnp.where(crow == first, NEG, cand)
        zsum = jnp.zeros((1, tm), F32)
        for t in tops[:PEER_TOPK]:
            zsum = zsum + jnp.exp(t - tops[0])
        tau = 0.5 * (tops[PEER_TOPK - 1] + tops[PEER_TOPK])
        m1 = v1[0:1, :]
        m2 = v2[0:1, :]
        e1_ref[h] = jnp.exp(s1 - m1) / zsum
        e2_ref[h] = jnp.exp(s2 - m2)
        th_ref[h] = jnp.exp(jnp.minimum(tau - m2 - s1, 80.0))


def _peer_topk(u2, w_q, k1, k2):
    T, D = u2.shape
    tm = 512
    out = jax.ShapeDtypeStruct((PEER_HEADS, PEER_N_KEYS, T), F32)
    ospec = pl.BlockSpec((PEER_HEADS, PEER_N_KEYS, tm), lambda i: (0, 0, i))
    return pl.pallas_call(
        _peer_topk_kernel,
        out_shape=(out, out, out),
        grid=(T // tm,),
        in_specs=[pl.BlockSpec((tm, D), lambda i: (i, 0)),
                  pl.BlockSpec(w_q.shape, lambda i: (0, 0)),
                  pl.BlockSpec(k1.shape, lambda i: (0, 0)),
                  pl.BlockSpec(k2.shape, lambda i: (0, 0))],
        out_specs=(ospec, ospec, ospec),
        scratch_shapes=[pltpu.VMEM((2, PEER_VPAD, tm), F32)],
        compiler_params=_cparams(("arbitrary",)),
        name="peer_topk",
    )(u2, w_q, k1, k2)


PEER_TE = 1024


def _peer_mix_kernel(u_ref, x1_ref, mod_ref, e1_ref, th_ref, e2_ref, eu_ref, ev_ref, o_ref,
                     h_ref, hid_ref, acc_ref):
    j = pl.program_id(1)
    tm = u_ref.shape[0]

    @pl.when(j == 0)
    def _():
        acc_ref[...] = jnp.zeros_like(acc_ref)

    h_ref[...] = _dot_nt(eu_ref[...], u_ref[...])
    for ii in range(PEER_TE // PEER_N_KEYS):
        rs = slice(ii * PEER_N_KEYS, (ii + 1) * PEER_N_KEYS)
        for tc in range(tm // LANE):
            cs = slice(tc * LANE, (tc + 1) * LANE)
            gate = jnp.zeros((PEER_N_KEYS, LANE), F32)
            for h in range(PEER_HEADS):
                e2 = e2_ref[h, :, cs]
                sel = jnp.where(e2 >= th_ref[h, ii:ii + 1, cs], e2, 0.0)
                gate = gate + sel * e1_ref[h, ii:ii + 1, cs]
            hb = h_ref[rs, cs]
            act = 0.5 * hb * (1.0 + lax.erf(hb * (2.0 ** -0.5)))
            hid_ref[rs, cs] = (act * gate).astype(BF16)
    acc_ref[...] += _dot_tn(hid_ref[...], ev_ref[...])

    @pl.when(j == pl.num_programs(1) - 1)
    def _():
        o_ref[...] = x1_ref[...] + mod_ref[0, 5:6, :] * acc_ref[...]


def _peer_mix(u2, x1, mod, e1, th, e2, exp_u, exp_v, S):
    T, D = u2.shape
    tm = 512
    te = PEER_TE
    n1 = te // PEER_N_KEYS
    tpb = S // tm
    ne = exp_u.shape[0] // te
    td = pl.BlockSpec((tm, D), lambda i, j: (i, 0))
    return pl.pallas_call(
        _peer_mix_kernel,
        out_shape=jax.ShapeDtypeStruct((T, D), F32),
        grid=(T // tm, ne),
        in_specs=[td, td, pl.BlockSpec((1, 6, D), lambda i, j: (i // tpb, 0, 0)),
                  pl.BlockSpec((PEER_HEADS, n1, tm), lambda i, j: (0, j, i)),
                  pl.BlockSpec((PEER_HEADS, n1, tm), lambda i, j: (0, j, i)),
                  pl.BlockSpec((PEER_HEADS, PEER_N_KEYS, tm), lambda i, j: (0, 0, i)),
                  pl.BlockSpec((te, D), lambda i, j: (j, 0)),
                  pl.BlockSpec((te, D), lambda i, j: (j, 0))],
        out_specs=td,
        scratch_shapes=[pltpu.VMEM((te, tm), F32), pltpu.VMEM((te, tm), BF16), pltpu.VMEM((tm, D), F32)],
        compiler_params=_cparams(("arbitrary", "arbitrary")),
        name="peer_mix",
    )(u2, x1, mod, e1, th, e2, exp_u, exp_v)


def _pad_cols(w, n):
    return jnp.pad(w, ((0, 0), (0, n - w.shape[1])))


def _pack_in_weights(w_in):
    na = 3 * NA_WIDTH
    rk = 3 * RWKV_WIDTH
    za, zb, zg = w_in[:, :na], w_in[:, na:na + rk + 640], w_in[:, na + rk + 640:]
    lora = zb[:, rk:]
    parts, start = [], 0
    for s in (DECAY_LORA, DECAY_LORA, AAA_LORA, AAA_LORA):
        parts.append(_pad_cols(lora[:, start:start + s], LANE))
        start += s
    parts.append(lora[:, start:start + GATE_LORA])
    lora_p = _pad_cols(jnp.concatenate(parts, axis=1), LORA_W)
    return jnp.concatenate([za, zb[:, :rk], zg, lora_p], axis=1).astype(BF16)


def _pack_mu_lora(mu):
    lora = mu[:, 3 * RWKV_WIDTH:]
    parts, start = [], 0
    for s in (DECAY_LORA, DECAY_LORA, AAA_LORA, AAA_LORA):
        parts.append(_pad_cols(lora[:, start:start + s], LANE))
        start += s
    parts.append(lora[:, start:start + GATE_LORA])
    return _pad_cols(jnp.concatenate(parts, axis=1), LORA_W)


def _block(x, c, ada_w, ada_b, norm1_w, norm2_w, w_in, b_gate, na_q_norm_w, na_k_norm_w, na_rpb,
           rwkv_mu, rwkv_w0, rwkv_w2, rwkv_a0, rwkv_a2, rwkv_g2, rwkv_k_k, rwkv_k_a, rwkv_r_k,
           rwkv_ln_w, rwkv_ln_b, w_branch_a, w_branch_b, w_out, peer_w_q, peer_sub_k1, peer_sub_k2,
           peer_u, peer_v):
    B, S, D = x.shape
    T = B * S
    x2 = x.reshape(T, D)
    mod = _adaln(c, ada_w, ada_b)

    z = _inproj(x2, mod, norm1_w, _pack_in_weights(w_in), S)

    bias = _na_bias_tables(na_rpb, S // GRID_W)
    o_a = _na(z, bias, na_q_norm_w, na_k_norm_w, B, S)

    head = jnp.arange(RWKV_WIDTH) // RWKV_HEAD_DIM
    headsum = (head[:, None] == head[None, :]).astype(BF16)
    w2p = jnp.pad(rwkv_w2, ((0, 0), (0, LANE - DECAY_LORA), (0, 0))).astype(BF16)
    a2p = jnp.pad(rwkv_a2, ((0, 0), (0, LANE - AAA_LORA), (0, 0))).astype(BF16)
    r, k, v, kk, iclr, logw, g = _rwkv_prep(
        z, rwkv_mu[:, :3 * RWKV_WIDTH], _pack_mu_lora(rwkv_mu), rwkv_w0, w2p, rwkv_a0, a2p,
        rwkv_g2.astype(BF16), rwkv_k_k, headsum, S)
    o2 = _rwkv_scan(r, k, v, kk, iclr, logw, rwkv_k_a, B, S)

    merged = _merge(o2, r, k, v, g, o_a, z, headsum, rwkv_r_k.reshape(-1), rwkv_ln_w, rwkv_ln_b,
                    b_gate, w_branch_a.astype(BF16), w_branch_b.astype(BF16), D)
    x1, u2 = _outproj(merged, x2, mod, norm2_w, w_out.astype(BF16), S)

    e1, th, e2 = _peer_topk(u2, peer_w_q.astype(BF16), peer_sub_k1.astype(BF16), peer_sub_k2.astype(BF16))
    out = _peer_mix(u2, x1, mod, e1, th, e2, peer_u.astype(BF16), peer_v.astype(BF16), S)
    return out.reshape(B, S, D)


def kernel(x, c, ada_w, ada_b, norm1_w, norm2_w, w_in, b_gate, na_q_norm_w, na_k_norm_w, na_rpb, rwkv_mu, rwkv_w0, rwkv_w2, rwkv_a0, rwkv_a2, rwkv_g2, rwkv_k_k, rwkv_k_a, rwkv_r_k, rwkv_ln_w, rwkv_ln_b, w_branch_a, w_branch_b, w_out, peer_w_q, peer_sub_k1, peer_sub_k2, peer_u, peer_v):
    h = x
    for layer in range(ada_w.shape[0]):
        h = _block(h, c, ada_w[layer], ada_b[layer], norm1_w[layer], norm2_w[layer], w_in[layer],
                   b_gate[layer], na_q_norm_w[layer], na_k_norm_w[layer], na_rpb[layer],
                   rwkv_mu[layer], rwkv_w0[layer], rwkv_w2[layer], rwkv_a0[layer], rwkv_a2[layer],
                   rwkv_g2[layer], rwkv_k_k[layer], rwkv_k_a[layer], rwkv_r_k[layer],
                   rwkv_ln_w[layer], rwkv_ln_b[layer], w_branch_a[layer], w_branch_b[layer],
                   w_out[layer], peer_w_q[layer], peer_sub_k1[layer], peer_sub_k2[layer],
                   peer_u[layer], peer_v[layer])
    return h
```

```python
import functools

import jax
import jax.numpy as jnp
from jax import lax
from jax.experimental import pallas as pl
from jax.experimental.pallas import tpu as pltpu

F32 = jnp.float32
BF16 = jnp.bfloat16

GRID_W = 64
NA_HEADS = 8
NA_HEAD_DIM = 128
NA_WIDTH = NA_HEADS * NA_HEAD_DIM
NA_KH = 8
NA_KW = 16
RWKV_HEAD_DIM = 64
RWKV_HEADS = 16
RWKV_WIDTH = RWKV_HEADS * RWKV_HEAD_DIM
DECAY_LORA = 96
AAA_LORA = 96
GATE_LORA = 256
RWKV_GN_EPS = 64e-5
PEER_HEADS = 8
PEER_DQ = 256
PEER_N_KEYS = 128
PEER_TOPK = 16
NORM_EPS = 1e-6

LANE = 128
VMEM_LIMIT = 56 * 1024 * 1024
NEG = -1e30

COL_QKV = 0
COL_RKV = 3 * NA_WIDTH
COL_GATE = COL_RKV + 3 * RWKV_WIDTH
COL_LORA = COL_GATE + 4096
IN_PAD = COL_LORA + 1024
LORA_W = 1024

RW_CHUNK = 64


def _cparams(sem):
    return pltpu.CompilerParams(dimension_semantics=sem, vmem_limit_bytes=VMEM_LIMIT)


def _dot(a, b):
    return jnp.dot(a, b, preferred_element_type=F32)


def _dot_nt(a, b):
    return lax.dot_general(a, b, (((1,), (1,)), ((), ())), preferred_element_type=F32)


def _dot_tn(a, b):
    return lax.dot_general(a, b, (((0,), (0,)), ((), ())), preferred_element_type=F32)


def _head_sums(x, hs):
    hi = x.astype(BF16)
    lo = (x - hi.astype(F32)).astype(BF16)
    outs = []
    for g in range(x.shape[1] // LANE):
        sl = slice(g * LANE, (g + 1) * LANE)
        outs.append(_dot(hi[:, sl], hs) + _dot(lo[:, sl], hs))
    return jnp.concatenate(outs, axis=1)


def _adaln_kernel(c_ref, w_ref, b_ref, o_ref):
    c = c_ref[...]
    s = c * jax.nn.sigmoid(c)
    o_ref[...] = jnp.dot(s, w_ref[...], preferred_element_type=F32,
                         precision=lax.Precision.HIGHEST) + b_ref[...]


def _adaln(c, ada_w, ada_b):
    B, D = c.shape
    n = ada_w.shape[1]
    tn = 1024
    cp = jnp.zeros((8, D), F32).at[:B].set(c)
    out = pl.pallas_call(
        _adaln_kernel,
        out_shape=jax.ShapeDtypeStruct((8, n), F32),
        grid=(n // tn,),
        in_specs=[pl.BlockSpec((8, D), lambda j: (0, 0)),
                  pl.BlockSpec((D, tn), lambda j: (0, j)),
                  pl.BlockSpec((1, tn), lambda j: (0, j))],
        out_specs=pl.BlockSpec((8, tn), lambda j: (0, j)),
        compiler_params=_cparams(("arbitrary",)),
        name="adaln",
    )(cp, ada_w, ada_b.reshape(1, n))
    return out[:B].reshape(B, 6, D)


def _inproj_kernel(x_ref, mod_ref, nw_ref, w_ref, o_ref, u_ref):
    @pl.when(pl.program_id(1) == 0)
    def _():
        x = x_ref[...]
        ms = jnp.mean(x * x, axis=-1, keepdims=True)
        y = x * lax.rsqrt(ms + NORM_EPS) * nw_ref[...]
        u = y * (1.0 + mod_ref[0, 1:2, :]) + mod_ref[0, 0:1, :]
        u_ref[...] = u.astype(BF16)

    o_ref[...] = _dot(u_ref[...], w_ref[...])


def _inproj(x2, mod, norm_w, w_pack, S):
    T, D = x2.shape
    n = w_pack.shape[1]
    tm = min(1024, S)
    tn = 1024
    tpb = S // tm
    return pl.pallas_call(
        _inproj_kernel,
        out_shape=jax.ShapeDtypeStruct((T, n), F32),
        grid=(T // tm, n // tn),
        in_specs=[pl.BlockSpec((tm, D), lambda i, j: (i, 0)),
                  pl.BlockSpec((1, 6, D), lambda i, j: (i // tpb, 0, 0)),
                  pl.BlockSpec((1, D), lambda i, j: (0, 0)),
                  pl.BlockSpec((D, tn), lambda i, j: (0, j))],
        out_specs=pl.BlockSpec((tm, tn), lambda i, j: (i, j)),
        scratch_shapes=[pltpu.VMEM((tm, D), BF16)],
        compiler_params=_cparams(("arbitrary", "arbitrary")),
        name="inproj",
    )(x2, mod, norm_w.reshape(1, D), w_pack)


NA_QROWS = 8
NA_BAND = 16
NA_TQ = NA_QROWS * GRID_W
NA_KBLK = 4 * GRID_W


def _na_bias_tables(rpb, rows):
    kh = NA_KH
    col = jnp.arange(GRID_W)
    col_start = jnp.clip(col - NA_KW // 2, 0, GRID_W - NA_KW)
    col_mask = (col[None, :] >= col_start[:, None]) & (col[None, :] < col_start[:, None] + NA_KW)
    dc = jnp.clip(col[None, :] - col[:, None], -(NA_KW - 1), NA_KW - 1) + NA_KW - 1
    hi = lax.Precision.HIGHEST
    col_tab = jnp.einsum('hdc,qkc->hdqk', rpb, jax.nn.one_hot(dc, 2 * NA_KW - 1, dtype=F32), precision=hi)
    tables = []
    for i0 in (0, min(NA_QROWS, rows - NA_QROWS), rows - NA_QROWS):
        b0 = min(max(i0 - 4, 0), rows - NA_BAND)
        i = i0 + jnp.arange(NA_QROWS)
        kr = b0 + jnp.arange(NA_BAND)
        rs = jnp.clip(i - kh // 2, 0, rows - kh)
        valid_r = (kr[None, :] >= rs[:, None]) & (kr[None, :] < rs[:, None] + kh)
        dr = jnp.clip(kr[None, :] - i[:, None] + NA_KH - 1, 0, 2 * NA_KH - 2)
        bias = jnp.einsum('jrd,hdqk->hjqrk', jax.nn.one_hot(dr, 2 * NA_KH - 1, dtype=F32), col_tab,
                          precision=hi)
        ok = valid_r[:, None, :, None] & col_mask[None, :, None, :]
        bias = jnp.where(ok[None], bias, NEG)
        tables.append(bias.reshape(rpb.shape[0], NA_TQ, NA_BAND * GRID_W))
    return jnp.stack(tables)


def _na_kernel(q_ref, k0, k1, k2, k3, v0, v1, v2, v3, bias_ref, qw_ref, kw_ref, o_ref):
    q = q_ref[...]
    qn = q * lax.rsqrt(jnp.mean(q * q, axis=-1, keepdims=True) + NORM_EPS) * qw_ref[...]
    qn = (qn * (NA_HEAD_DIM ** -0.5)).astype(BF16)
    logits = []
    for n, kr in enumerate((k0, k1, k2, k3)):
        k = kr[...]
        kn = (k * lax.rsqrt(jnp.mean(k * k, axis=-1, keepdims=True) + NORM_EPS) * kw_ref[...]).astype(BF16)
        logits.append(_dot_nt(qn, kn) + bias_ref[0, 0, :, n * NA_KBLK:(n + 1) * NA_KBLK])
    m = logits[0].max(axis=-1, keepdims=True)
    for l in logits[1:]:
        m = jnp.maximum(m, l.max(axis=-1, keepdims=True))
    den = jnp.zeros_like(m)
    acc = jnp.zeros(o_ref.shape, F32)
    for l, vr in zip(logits, (v0, v1, v2, v3)):
        p = jnp.exp(l - m)
        den = den + p.sum(axis=-1, keepdims=True)
        acc = acc + _dot(p.astype(BF16), vr[...].astype(BF16))
    o_ref[...] = acc / den


def _na(z, bias, qw, kw, B, S):
    T = z.shape[0]
    rows = S // GRID_W
    nblk = rows // NA_QROWS
    nkb = S // NA_KBLK

    def kv_map(col0, n):
        def f(b, h, i):
            start = jnp.clip(2 * i - 1, 0, nkb - 4)
            return (b * nkb + start + n, col0 + h)
        return f

    def bias_map(b, h, i):
        ty = jnp.where(i == 0, 0, jnp.where(i == nblk - 1, 2, 1))
        return (ty, h, 0, 0)

    q_spec = pl.BlockSpec((NA_TQ, LANE), lambda b, h, i: (b * nblk + i, h))
    k_specs = [pl.BlockSpec((NA_KBLK, LANE), kv_map(NA_HEADS, n)) for n in range(4)]
    v_specs = [pl.BlockSpec((NA_KBLK, LANE), kv_map(2 * NA_HEADS, n)) for n in range(4)]
    w_spec = pl.BlockSpec((1, LANE), lambda b, h, i: (0, 0))
    return pl.pallas_call(
        _na_kernel,
        out_shape=jax.ShapeDtypeStruct((T, NA_WIDTH), F32),
        grid=(B, NA_HEADS, nblk),
        in_specs=[q_spec, *k_specs, *v_specs,
                  pl.BlockSpec((1, 1, NA_TQ, NA_BAND * GRID_W), bias_map), w_spec, w_spec],
        out_specs=pl.BlockSpec((NA_TQ, LANE), lambda b, h, i: (b * nblk + i, h)),
        compiler_params=_cparams(("arbitrary", "arbitrary", "arbitrary")),
        name="na_attn",
    )(z, z, z, z, z, z, z, z, z, bias, qw.reshape(1, LANE), kw.reshape(1, LANE))


def _rwkv_prep_kernel(r_ref, k_ref, v_ref, l_ref,
                      rp_ref, kp_ref, vp_ref, lp_ref,
                      rn_ref, kn_ref, vn_ref, ln_ref,
                      mu_ref, mul_ref, w0_ref, w2_ref, a0_ref, a2_ref, g2_ref, kk_ref, hs_ref,
                      ro_ref, ko_ref, vo_ref, kko_ref, iclr_ref, logw_ref, g_ref, *, tpb):
    i = pl.program_id(0)
    first = (i % tpb) == 0
    last = (i % tpb) == tpb - 1
    tm = r_ref.shape[0]
    rows = lax.broadcasted_iota(jnp.int32, (tm, 1), 0)

    def shift(y_ref, p_ref, n_ref, mu0, mu1):
        y = y_ref[...]
        prow = jnp.where(first, 0.0, p_ref[7:8, :])
        nrow = jnp.where(last, 0.0, n_ref[0:1, :])
        prev = jnp.where(rows == 0, prow, pltpu.roll(y, 1, 0))
        nxt = jnp.where(rows == tm - 1, nrow, pltpu.roll(y, tm - 1, 0))
        return y + mu0 * (prev - y) + mu1 * (nxt - y)

    r = shift(r_ref, rp_ref, rn_ref, mu_ref[0:1, 0:1024], mu_ref[1:2, 0:1024])
    k = shift(k_ref, kp_ref, kn_ref, mu_ref[0:1, 1024:2048], mu_ref[1:2, 1024:2048])
    v = shift(v_ref, vp_ref, vn_ref, mu_ref[0:1, 2048:3072], mu_ref[1:2, 2048:3072])
    lo = shift(l_ref, lp_ref, ln_ref, mul_ref[0:1, :], mul_ref[1:2, :])
    ro_ref[...] = r
    ko_ref[...] = k
    vo_ref[...] = v

    for d in range(2):
        wd = jnp.tanh(lo[:, d * LANE:(d + 1) * LANE]).astype(BF16)
        w_logit = w0_ref[d:d + 1, :] + _dot(wd, w2_ref[d])
        x = -w_logit
        softplus = jnp.maximum(x, 0.0) + jnp.log(1.0 + jnp.exp(-jnp.abs(x)))
        logw_ref[d] = -jnp.exp(-softplus - 0.5)
        ad = lo[:, (2 + d) * LANE:(3 + d) * LANE].astype(BF16)
        iclr_ref[d] = jax.nn.sigmoid(a0_ref[d:d + 1, :] + _dot(ad, a2_ref[d]))
    gd = jax.nn.sigmoid(lo[:, 4 * LANE:4 * LANE + GATE_LORA]).astype(BF16)
    g_ref[...] = _dot(gd, g2_ref[...])

    kk = k * kk_ref[...]
    ss = _head_sums(kk * kk, hs_ref[...])
    kko_ref[...] = kk / jnp.maximum(jnp.sqrt(ss), 1e-12)


def _rwkv_prep(z, mu_rkv, mu_lora, w0, w2p, a0, a2p, g2, k_k, headsum, S):
    T = z.shape[0]
    C = RWKV_WIDTH
    tm = 256
    tpb = S // tm
    nt = T // tm
    hb = tm // 8
    cb = COL_RKV // C
    lb = COL_LORA // LORA_W

    def main(col):
        return pl.BlockSpec((tm, C), lambda i: (i, col))

    def prev(col):
        return pl.BlockSpec((8, C), lambda i: (jnp.maximum(i * hb - 1, 0), col))

    def nxt(col):
        return pl.BlockSpec((8, C), lambda i: (jnp.minimum((i + 1) * hb, T // 8 - 1), col))

    def full(shape):
        nd = len(shape)
        return pl.BlockSpec(shape, lambda i: (0,) * nd)

    cols = (cb, cb + 1, cb + 2, lb)
    out_tc = jax.ShapeDtypeStruct((T, C), F32)
    out_2tc = jax.ShapeDtypeStruct((2, T, C), F32)
    spec_tc = pl.BlockSpec((tm, C), lambda i: (i, 0))
    spec_2tc = pl.BlockSpec((2, tm, C), lambda i: (0, i, 0))
    return pl.pallas_call(
        functools.partial(_rwkv_prep_kernel, tpb=tpb),
        out_shape=(out_tc, out_tc, out_tc, out_tc, out_2tc, out_2tc, out_tc),
        grid=(nt,),
        in_specs=[*[main(c) for c in cols], *[prev(c) for c in cols], *[nxt(c) for c in cols],
                  full(mu_rkv.shape), full(mu_lora.shape), full(w0.shape), full(w2p.shape),
                  full(a0.shape), full(a2p.shape), full(g2.shape), full((1, C)), full(headsum.shape)],
        out_specs=(spec_tc, spec_tc, spec_tc, spec_tc, spec_2tc, spec_2tc, spec_tc),
        compiler_params=_cparams(("arbitrary",)),
        name="rwkv_prep",
    )(z, z, z, z, z, z, z, z, z, z, z, z,
      mu_rkv, mu_lora, w0, w2p, a0, a2p, g2, k_k.reshape(1, C), headsum)


def _rwkv_scan_kernel(rf_ref, kf_ref, vf_ref, kkf_ref, rb_ref, kb_ref, vb_ref, kkb_ref,
                      iclrf_ref, iclrb_ref, logwf_ref, logwb_ref, ka_ref, of_ref, ob_ref, s_ref):
    L = RW_CHUNK
    n_pairs = RWKV_WIDTH // LANE

    @pl.when(pl.program_id(1) == 0)
    def _():
        s_ref[...] = jnp.zeros_like(s_ref)

    ti = lax.broadcasted_iota(jnp.int32, (L, L), 0)
    si = lax.broadcasted_iota(jnp.int32, (L, L), 1)
    R = lax.broadcasted_iota(jnp.int32, (2 * L, 2 * L), 0)
    Cc = lax.broadcasted_iota(jnp.int32, (2 * L, 2 * L), 1)
    same = (R >= L) == (Cc >= L)
    eye = jnp.where(R == Cc, 1.0, 0.0)
    lane = lax.broadcasted_iota(jnp.int32, (1, LANE), 1)
    head0 = lane < RWKV_HEAD_DIM
    ones_l = jnp.ones((L, LANE), BF16)

    def stack(x):
        return jnp.concatenate([jnp.where(head0, x, 0.0), jnp.where(head0, 0.0, x)], axis=0)

    dirs = ((rf_ref, kf_ref, vf_ref, kkf_ref, iclrf_ref, logwf_ref, of_ref, 1),
            (rb_ref, kb_ref, vb_ref, kkb_ref, iclrb_ref, logwb_ref, ob_ref, -1))
    chains = []
    for d, (r_ref, k_ref, v_ref, kk_ref, iclr_ref, logw_ref, o_ref, sgn) in enumerate(dirs):
        cum_mask = jnp.where((ti - si) * sgn >= 0, 1.0, 0.0).astype(BF16)
        m_strict = same & ((R - Cc) * sgn > 0)
        m_incl = same & ((R - Cc) * sgn >= 0)
        lw = logw_ref[0]
        lw_hi = lw.astype(BF16)
        lw_lo = (lw - lw_hi.astype(F32)).astype(BF16)
        cs = _dot(cum_mask, lw_hi) + _dot(cum_mask, lw_lo)
        tot = _dot_tn(lw_hi, ones_l) + _dot_tn(lw_lo, ones_l)
        e_in = jnp.exp(cs)
        e_out = jnp.exp(-cs)
        kk = kk_ref[...]
        iclr = iclr_ref[0]
        a_t = -kk * jnp.exp(cs - lw)
        b_t = kk * iclr * e_out
        k_t = k_ref[...] * (1.0 + (iclr - 1.0) * ka_ref[...]) * e_out
        r_t = r_ref[...] * e_in
        v = v_ref[...]
        for p in range(n_pairs):
            sl = slice(p * LANE, (p + 1) * LANE)
            chains.append(dict(
                d=d, p=p, o_ref=o_ref, sl=sl, m_strict=m_strict, m_incl=m_incl,
                vs=stack(v[:, sl]).astype(BF16),
                ar=jnp.concatenate([stack(a_t[:, sl]), stack(r_t[:, sl])], axis=0).astype(BF16),
                bk=jnp.concatenate([stack(b_t[:, sl]), stack(k_t[:, sl])], axis=0).astype(BF16),
                decay=jnp.exp(tot[sl, :])))

    for c in chains:
        aa = _dot_nt(c["ar"], c["bk"])
        c["a_ab"] = jnp.where(c["m_strict"], aa[:2 * L, :2 * L], 0.0)
        c["a_ak"] = jnp.where(c["m_strict"], aa[:2 * L, 2 * L:], 0.0).astype(BF16)
        c["a_r"] = jnp.concatenate([jnp.where(c["m_incl"], aa[2 * L:, :2 * L], 0.0),
                                    jnp.where(c["m_incl"], aa[2 * L:, 2 * L:], 0.0)], axis=1).astype(BF16)
        c["pw"] = c["a_ab"].astype(BF16)
        c["tinv"] = eye + c["a_ab"]
    for _ in range(5):
        for c in chains:
            c["pw2"] = _dot(c["pw"], c["pw"]).astype(BF16)
        for c in chains:
            c["tinv"] = c["tinv"] + _dot(c["tinv"].astype(BF16), c["pw2"])
            c["pw"] = c["pw2"]
    for c in chains:
        c["sp"] = s_ref[c["d"], c["p"]]
        c["sa_sr"] = _dot(c["ar"], c["sp"].astype(BF16))
        c["akv"] = _dot(c["a_ak"], c["vs"])
    for c in chains:
        x1 = c["sa_sr"][:2 * L] + c["akv"]
        u = _dot(c["tinv"].astype(BF16), x1.astype(BF16))
        c["uv"] = jnp.concatenate([u.astype(BF16), c["vs"]], axis=0)
    for c in chains:
        o_s = c["sa_sr"][2 * L:] + _dot(c["a_r"], c["uv"])
        c["o_ref"][:, c["sl"]] = o_s[:L] + o_s[L:]
        s_ref[c["d"], c["p"]] = (c["sp"] + _dot_tn(c["bk"], c["uv"])) * c["decay"]


def _rwkv_scan(r, k, v, kk, iclr, logw, k_a, B, S):
    T, C = r.shape
    L = RW_CHUNK
    nc = S // L
    fwd = pl.BlockSpec((L, C), lambda b, c: (b * nc + c, 0))
    bwd = pl.BlockSpec((L, C), lambda b, c: (b * nc + nc - 1 - c, 0))
    fwd2 = pl.BlockSpec((1, L, C), lambda b, c: (0, b * nc + c, 0))
    bwd2 = pl.BlockSpec((1, L, C), lambda b, c: (1, b * nc + nc - 1 - c, 0))
    out = jax.ShapeDtypeStruct((T, C), F32)
    return pl.pallas_call(
        _rwkv_scan_kernel,
        out_shape=(out, out),
        grid=(B, nc),
        in_specs=[fwd, fwd, fwd, fwd, bwd, bwd, bwd, bwd, fwd2, bwd2, fwd2, bwd2,
                  pl.BlockSpec((1, C), lambda b, c: (0, 0))],
        out_specs=(fwd, bwd),
        scratch_shapes=[pltpu.VMEM((2, C // LANE, LANE, LANE), F32)],
        compiler_params=_cparams(("arbitrary", "arbitrary")),
        name="rwkv_scan",
    )(r, k, v, kk, r, k, v, kk, iclr, iclr, logw, logw, k_a.reshape(1, C))


def _merge_kernel(of_ref, ob_ref, r_ref, k_ref, v_ref, g_ref, oa_ref, za_ref, zb_ref,
                  hs_ref, rk_ref, lnw_ref, lnb_ref, bga_ref, bgb_ref, wa_ref, wb_ref,
                  out_ref, yb_ref, oab_ref):
    @pl.when(pl.program_id(1) == 0)
    def _():
        hs = hs_ref[...]
        o = of_ref[...] + ob_ref[...]
        inv_n = 1.0 / RWKV_HEAD_DIM
        mean = _head_sums(o, hs) * inv_n
        cen = o - mean
        var = _head_sums(cen * cen, hs) * inv_n
        y = cen * lax.rsqrt(var + RWKV_GN_EPS) * lnw_ref[...] + lnb_ref[...]
        v = v_ref[...]
        bonus = _head_sums(r_ref[...] * k_ref[...] * rk_ref[...], hs) * v
        yb_ref[...] = ((y + bonus) * g_ref[...]).astype(BF16)
        oab_ref[...] = oa_ref[...].astype(BF16)

    ga = jax.nn.sigmoid(za_ref[...] + bga_ref[...])
    gb = jax.nn.sigmoid(zb_ref[...] + bgb_ref[...])
    out_ref[...] = ga * _dot(oab_ref[...], wa_ref[...]) + gb * _dot(yb_ref[...], wb_ref[...])


def _merge(o_f, o_b, r, k, v, g, oa, z, headsum, r_k, ln_w, ln_b, b_gate, wa, wb, D):
    T, C = r.shape
    tm, tn = 256, 1024
    nj = D // tn
    gcol = COL_GATE // tn
    tc = pl.BlockSpec((tm, C), lambda i, j: (i, 0))

    def vec(n):
        return pl.BlockSpec((1, n), lambda i, j: (0, 0))

    return pl.pallas_call(
        _merge_kernel,
        out_shape=jax.ShapeDtypeStruct((T, D), F32),
        grid=(T // tm, nj),
        in_specs=[tc, tc, tc, tc, tc, tc, tc,
                  pl.BlockSpec((tm, tn), lambda i, j: (i, gcol + j)),
                  pl.BlockSpec((tm, tn), lambda i, j: (i, gcol + nj + j)),
                  pl.BlockSpec((LANE, LANE), lambda i, j: (0, 0)), vec(C), vec(C), vec(C),
                  pl.BlockSpec((1, tn), lambda i, j: (0, j)),
                  pl.BlockSpec((1, tn), lambda i, j: (0, nj + j)),
                  pl.BlockSpec((C, tn), lambda i, j: (0, j)),
                  pl.BlockSpec((C, tn), lambda i, j: (0, j))],
        out_specs=pl.BlockSpec((tm, tn), lambda i, j: (i, j)),
        scratch_shapes=[pltpu.VMEM((tm, C), BF16), pltpu.VMEM((tm, C), BF16)],
        compiler_params=_cparams(("arbitrary", "arbitrary")),
        name="merge",
    )(o_f, o_b, r, k, v, g, oa, z, z, headsum, r_k.reshape(1, C), ln_w.reshape(1, C), ln_b.reshape(1, C),
      b_gate.reshape(1, 2 * D), b_gate.reshape(1, 2 * D), wa, wb)


def _outproj_kernel(m_ref, x_ref, mod_ref, nw_ref, w_ref, x1_ref, u2_ref):
    x1 = x_ref[...] + mod_ref[0, 2:3, :] * _dot(m_ref[...].astype(BF16), w_ref[...])
    x1_ref[...] = x1
    ms = jnp.mean(x1 * x1, axis=-1, keepdims=True)
    y = x1 * lax.rsqrt(ms + NORM_EPS) * nw_ref[...]
    u2_ref[...] = (y * (1.0 + mod_ref[0, 4:5, :]) + mod_ref[0, 3:4, :]).astype(BF16)


def _outproj(merged, x2, mod, norm_w, w_out, S):
    T, D = x2.shape
    tm = 256
    tpb = S // tm
    td = pl.BlockSpec((tm, D), lambda i: (i, 0))
    return pl.pallas_call(
        _outproj_kernel,
        out_shape=(jax.ShapeDtypeStruct((T, D), F32), jax.ShapeDtypeStruct((T, D), BF16)),
        grid=(T // tm,),
        in_specs=[td, td, pl.BlockSpec((1, 6, D), lambda i: (i // tpb, 0, 0)),
                  pl.BlockSpec((1, D), lambda i: (0, 0)), pl.BlockSpec((D, D), lambda i: (0, 0))],
        out_specs=(td, td),
        compiler_params=_cparams(("arbitrary",)),
        name="outproj",
    )(merged, x2, mod, norm_w.reshape(1, D), w_out)


PEER_NV = PEER_TOPK + 1
PEER_VPAD = 24


def _peer_topk_kernel(u_ref, wq_ref, k1_ref, k2_ref, e1_ref, th_ref, e2_ref, vals_ref):
    tm = u_ref.shape[0]
    half = PEER_DQ // 2
    q = _dot(u_ref[...], wq_ref[...])
    vals_ref[...] = jnp.full(vals_ref.shape, NEG, F32)
    for h in range(PEER_HEADS):
        q1 = q[:, h * PEER_DQ:h * PEER_DQ + half].astype(BF16)
        q2 = q[:, h * PEER_DQ + half:(h + 1) * PEER_DQ].astype(BF16)
        s1 = _dot_nt(k1_ref[...], q1)
        s2 = _dot_nt(k2_ref[...], q2)
        for side, s in enumerate((s1, s2)):
            for it in range(PEER_NV):
                m = s.max(axis=0, keepdims=True)
                vals_ref[side, it:it + 1, :] = m
                s = jnp.where(s >= m, NEG, s)
        v1 = vals_ref[0]
        v2 = vals_ref[1]
        cands = [v1[a:a + 1, :] + v2[0:8, :] for a in range(8)]
        cands.append(v1[0:1, :] + v2[8:PEER_VPAD, :])
        cands.append(v1[8:PEER_VPAD, :] + v2[0:1, :])
        cand = jnp.concatenate(cands, axis=0)
        crow = lax.broadcasted_iota(jnp.int32, cand.shape, 0)
        tops = []
        for it in range(PEER_NV):
            m = cand.max(axis=0, keepdims=True)
            tops.append(m)
            first = jnp.where(cand >= m, crow, cand.shape[0]).min(axis=0, keepdims=True)
            cand = jnp.where(crow == first, NEG, cand)
        zsum = jnp.zeros((1, tm), F32)
        for t in tops[:PEER_TOPK]:
            zsum = zsum + jnp.exp(t - tops[0])
        tau = 0.5 * (tops[PEER_TOPK - 1] + tops[PEER_TOPK])
        m1 = v1[0:1, :]
        m2 = v2[0:1, :]
        e1_ref[h] = jnp.exp(s1 - m1) / zsum
        e2_ref[h] = jnp.exp(s2 - m2)
        th_ref[h] = jnp.exp(jnp.minimum(tau - m2 - s1, 80.0))


def _peer_topk(u2, w_q, k1, k2):
    T, D = u2.shape
    tm = 512
    out = jax.ShapeDtypeStruct((PEER_HEADS, PEER_N_KEYS, T), F32)
    ospec = pl.BlockSpec((PEER_HEADS, PEER_N_KEYS, tm), lambda i: (0, 0, i))
    return pl.pallas_call(
        _peer_topk_kernel,
        out_shape=(out, out, out),
        grid=(T // tm,),
        in_specs=[pl.BlockSpec((tm, D), lambda i: (i, 0)),
                  pl.BlockSpec(w_q.shape, lambda i: (0, 0)),
                  pl.BlockSpec(k1.shape, lambda i: (0, 0)),
                  pl.BlockSpec(k2.shape, lambda i: (0, 0))],
        out_specs=(ospec, ospec, ospec),
        scratch_shapes=[pltpu.VMEM((2, PEER_VPAD, tm), F32)],
        compiler_params=_cparams(("arbitrary",)),
        name="peer_topk",
    )(u2, w_q, k1, k2)


PEER_TE = 1024


def _peer_mix_kernel(u_ref, x1_ref, mod_ref, e1_ref, th_ref, e2_ref, eu_ref, ev_ref, o_ref,
                     h_ref, hid_ref, acc_ref):
    j = pl.program_id(1)
    tm = u_ref.shape[0]

    @pl.when(j == 0)
    def _():
        acc_ref[...] = jnp.zeros_like(acc_ref)

    h_ref[...] = _dot_nt(eu_ref[...], u_ref[...])
    for ii in range(PEER_TE // PEER_N_KEYS):
        rs = slice(ii * PEER_N_KEYS, (ii + 1) * PEER_N_KEYS)
        for tc in range(tm // LANE):
            cs = slice(tc * LANE, (tc + 1) * LANE)
            gate = jnp.zeros((PEER_N_KEYS, LANE), F32)
            for h in range(PEER_HEADS):
                e2 = e2_ref[h, :, cs]
                sel = jnp.where(e2 >= th_ref[h, ii:ii + 1, cs], e2, 0.0)
                gate = gate + sel * e1_ref[h, ii:ii + 1, cs]
            hb = h_ref[rs, cs]
            act = 0.5 * hb * (1.0 + lax.erf(hb * (2.0 ** -0.5)))
            hid_ref[rs, cs] = (act * gate).astype(BF16)
    acc_ref[...] += _dot_tn(hid_ref[...], ev_ref[...])

    @pl.when(j == pl.num_programs(1) - 1)
    def _():
        o_ref[...] = x1_ref[...] + mod_ref[0, 5:6, :] * acc_ref[...]


def _peer_mix(u2, x1, mod, e1, th, e2, exp_u, exp_v, S):
    T, D = u2.shape
    tm = 512
    te = PEER_TE
    n1 = te // PEER_N_KEYS
    tpb = S // tm
    ne = exp_u.shape[0] // te
    td = pl.BlockSpec((tm, D), lambda i, j: (i, 0))
    return pl.pallas_call(
        _peer_mix_kernel,
        out_shape=jax.ShapeDtypeStruct((T, D), F32),
        grid=(T // tm, ne),
        in_specs=[td, td, pl.BlockSpec((1, 6, D), lambda i, j: (i // tpb, 0, 0)),
                  pl.BlockSpec((PEER_HEADS, n1, tm), lambda i, j: (0, j, i)),
                  pl.BlockSpec((PEER_HEADS, n1, tm), lambda i, j: (0, j, i)),
                  pl.BlockSpec((PEER_HEADS, PEER_N_KEYS, tm), lambda i, j: (0, 0, i)),
                  pl.BlockSpec((te, D), lambda i, j: (j, 0)),
                  pl.BlockSpec((te, D), lambda i, j: (j, 0))],
        out_specs=td,
        scratch_shapes=[pltpu.VMEM((te, tm), F32), pltpu.VMEM((te, tm), BF16), pltpu.VMEM((tm, D), F32)],
        compiler_params=_cparams(("arbitrary", "arbitrary")),
        name="peer_mix",
    )(u2, x1, mod, e1, th, e2, exp_u, exp_v)


def _pad_cols(w, n):
    return jnp.pad(w, ((0, 0), (0, n - w.shape[1])))


def _pack_in_weights(w_in):
    na = 3 * NA_WIDTH
    rk = 3 * RWKV_WIDTH
    za, zb, zg = w_in[:, :na], w_in[:, na:na + rk + 640], w_in[:, na + rk + 640:]
    lora = zb[:, rk:]
    parts, start = [], 0
    for s in (DECAY_LORA, DECAY_LORA, AAA_LORA, AAA_LORA):
        parts.append(_pad_cols(lora[:, start:start + s], LANE))
        start += s
    parts.append(lora[:, start:start + GATE_LORA])
    lora_p = _pad_cols(jnp.concatenate(parts, axis=1), LORA_W)
    return jnp.concatenate([za, zb[:, :rk], zg, lora_p], axis=1).astype(BF16)


def _pack_mu_lora(mu):
    lora = mu[:, 3 * RWKV_WIDTH:]
    parts, start = [], 0
    for s in (DECAY_LORA, DECAY_LORA, AAA_LORA, AAA_LORA):
        parts.append(_pad_cols(lora[:, start:start + s], LANE))
        start += s
    parts.append(lora[:, start:start + GATE_LORA])
    return _pad_cols(jnp.concatenate(parts, axis=1), LORA_W)


def _block(x, c, ada_w, ada_b, norm1_w, norm2_w, w_in, b_gate, na_q_norm_w, na_k_norm_w, na_rpb,
           rwkv_mu, rwkv_w0, rwkv_w2, rwkv_a0, rwkv_a2, rwkv_g2, rwkv_k_k, rwkv_k_a, rwkv_r_k,
           rwkv_ln_w, rwkv_ln_b, w_branch_a, w_branch_b, w_out, peer_w_q, peer_sub_k1, peer_sub_k2,
           peer_u, peer_v):
    B, S, D = x.shape
    T = B * S
    x2 = x.reshape(T, D)
    mod = _adaln(c, ada_w, ada_b)

    z = _inproj(x2, mod, norm1_w, _pack_in_weights(w_in), S)

    bias = _na_bias_tables(na_rpb, S // GRID_W)
    o_a = _na(z, bias, na_q_norm_w, na_k_norm_w, B, S)

    head = jnp.arange(LANE) // RWKV_HEAD_DIM
    headsum = (head[:, None] == head[None, :]).astype(BF16)
    w2p = jnp.pad(rwkv_w2, ((0, 0), (0, LANE - DECAY_LORA), (0, 0))).astype(BF16)
    a2p = jnp.pad(rwkv_a2, ((0, 0), (0, LANE - AAA_LORA), (0, 0))).astype(BF16)
    r, k, v, kk, iclr, logw, g = _rwkv_prep(
        z, rwkv_mu[:, :3 * RWKV_WIDTH], _pack_mu_lora(rwkv_mu), rwkv_w0, w2p, rwkv_a0, a2p,
        rwkv_g2.astype(BF16), rwkv_k_k, headsum, S)
    o_f, o_b = _rwkv_scan(r, k, v, kk, iclr, logw, rwkv_k_a, B, S)

    merged = _merge(o_f, o_b, r, k, v, g, o_a, z, headsum, rwkv_r_k.reshape(-1), rwkv_ln_w, rwkv_ln_b,
                    b_gate, w_branch_a.astype(BF16), w_branch_b.astype(BF16), D)
    x1, u2 = _outproj(merged, x2, mod, norm2_w, w_out.astype(BF16), S)

    e1, th, e2 = _peer_topk(u2, peer_w_q.astype(BF16), peer_sub_k1.astype(BF16), peer_sub_k2.astype(BF16))
    out = _peer_mix(u2, x1, mod, e1, th, e2, peer_u.astype(BF16), peer_v.astype(BF16), S)
    return out.reshape(B, S, D)


def kernel(x, c, ada_w, ada_b, norm1_w, norm2_w, w_in, b_gate, na_q_norm_w, na_k_norm_w, na_rpb, rwkv_mu, rwkv_w0, rwkv_w2, rwkv_a0, rwkv_a2, rwkv_g2, rwkv_k_k, rwkv_k_a, rwkv_r_k, rwkv_ln_w, rwkv_ln_b, w_branch_a, w_branch_b, w_out, peer_w_q, peer_sub_k1, peer_sub_k2, peer_u, peer_v):
    h = x
    for layer in range(ada_w.shape[0]):
        h = _block(h, c, ada_w[layer], ada_b[layer], norm1_w[layer], norm2_w[layer], w_in[layer],
                   b_gate[layer], na_q_norm_w[layer], na_k_norm_w[layer], na_rpb[layer],
                   rwkv_mu[layer], rwkv_w0[layer], rwkv_w2[layer], rwkv_a0[layer], rwkv_a2[layer],
                   rwkv_g2[layer], rwkv_k_k[layer], rwkv_k_a[layer], rwkv_r_k[layer],
                   rwkv_ln_w[layer], rwkv_ln_b[layer], w_branch_a[layer], w_branch_b[layer],
                   w_out[layer], peer_w_q[layer], peer_sub_k1[layer], peer_sub_k2[layer],
                   peer_u[layer], peer_v[layer])
    return h
```

```python
import functools

import jax
import jax.numpy as jnp
from jax import lax
from jax.experimental import pallas as pl
from jax.experimental.pallas import tpu as pltpu

F32 = jnp.float32
BF16 = jnp.bfloat16

GRID_W = 64
NA_HEADS = 8
NA_HEAD_DIM = 128
NA_WIDTH = NA_HEADS * NA_HEAD_DIM
NA_KH = 8
NA_KW = 16
RWKV_HEAD_DIM = 64
RWKV_HEADS = 16
RWKV_WIDTH = RWKV_HEADS * RWKV_HEAD_DIM
DECAY_LORA = 96
AAA_LORA = 96
GATE_LORA = 256
RWKV_GN_EPS = 64e-5
PEER_HEADS = 8
PEER_DQ = 256
PEER_N_KEYS = 128
PEER_TOPK = 16
NORM_EPS = 1e-6

LANE = 128
VMEM_LIMIT = 56 * 1024 * 1024
NEG = -1e30

COL_QKV = 0
COL_RKV = 3 * NA_WIDTH
COL_GATE = COL_RKV + 3 * RWKV_WIDTH
COL_LORA = COL_GATE + 4096
IN_PAD = COL_LORA + 1024
LORA_W = 1024

RW_CHUNK = 64


def _cparams(sem):
    return pltpu.CompilerParams(dimension_semantics=sem, vmem_limit_bytes=VMEM_LIMIT)


def _dot(a, b):
    return jnp.dot(a, b, preferred_element_type=F32)


def _dot_nt(a, b):
    return lax.dot_general(a, b, (((1,), (1,)), ((), ())), preferred_element_type=F32)


def _dot_tn(a, b):
    return lax.dot_general(a, b, (((0,), (0,)), ((), ())), preferred_element_type=F32)


def _head_sums(x, hs):
    hi = x.astype(BF16)
    lo = (x - hi.astype(F32)).astype(BF16)
    outs = []
    for g in range(x.shape[1] // LANE):
        sl = slice(g * LANE, (g + 1) * LANE)
        outs.append(_dot(hi[:, sl], hs) + _dot(lo[:, sl], hs))
    return jnp.concatenate(outs, axis=1)


def _adaln_kernel(c_ref, w_ref, b_ref, o_ref):
    c = c_ref[...]
    s = c * jax.nn.sigmoid(c)
    o_ref[...] = jnp.dot(s, w_ref[...], preferred_element_type=F32,
                         precision=lax.Precision.HIGHEST) + b_ref[...]


def _adaln(c, ada_w, ada_b):
    B, D = c.shape
    n = ada_w.shape[1]
    tn = 1024
    cp = jnp.zeros((8, D), F32).at[:B].set(c)
    out = pl.pallas_call(
        _adaln_kernel,
        out_shape=jax.ShapeDtypeStruct((8, n), F32),
        grid=(n // tn,),
        in_specs=[pl.BlockSpec((8, D), lambda j: (0, 0)),
                  pl.BlockSpec((D, tn), lambda j: (0, j)),
                  pl.BlockSpec((1, tn), lambda j: (0, j))],
        out_specs=pl.BlockSpec((8, tn), lambda j: (0, j)),
        compiler_params=_cparams(("arbitrary",)),
        name="adaln",
    )(cp, ada_w, ada_b.reshape(1, n))
    return out[:B].reshape(B, 6, D)


def _inproj_kernel(x_ref, mod_ref, nw_ref, w_ref, o_ref, u_ref):
    @pl.when(pl.program_id(1) == 0)
    def _():
        x = x_ref[...]
        ms = jnp.mean(x * x, axis=-1, keepdims=True)
        y = x * lax.rsqrt(ms + NORM_EPS) * nw_ref[...]
        u = y * (1.0 + mod_ref[0, 1:2, :]) + mod_ref[0, 0:1, :]
        u_ref[...] = u.astype(BF16)

    o_ref[...] = _dot(u_ref[...], w_ref[...])


def _inproj(x2, mod, norm_w, w_pack, S):
    T, D = x2.shape
    n = w_pack.shape[1]
    tm = min(1024, S)
    tn = 1024
    tpb = S // tm
    return pl.pallas_call(
        _inproj_kernel,
        out_shape=jax.ShapeDtypeStruct((T, n), F32),
        grid=(T // tm, n // tn),
        in_specs=[pl.BlockSpec((tm, D), lambda i, j: (i, 0)),
                  pl.BlockSpec((1, 6, D), lambda i, j: (i // tpb, 0, 0)),
                  pl.BlockSpec((1, D), lambda i, j: (0, 0)),
                  pl.BlockSpec((D, tn), lambda i, j: (0, j))],
        out_specs=pl.BlockSpec((tm, tn), lambda i, j: (i, j)),
        scratch_shapes=[pltpu.VMEM((tm, D), BF16)],
        compiler_params=_cparams(("arbitrary", "arbitrary")),
        name="inproj",
    )(x2, mod, norm_w.reshape(1, D), w_pack)


NA_QROWS = 8
NA_BAND = 16
NA_TQ = NA_QROWS * GRID_W
NA_KBLK = 4 * GRID_W


NA_ROW_OUT = 2 * NA_KH - 1


def _na_pair_plan(rows):
    combos, plan = {}, []
    for i0 in (0, min(NA_QROWS, rows - NA_QROWS), rows - NA_QROWS):
        b0 = min(max(i0 - NA_KH // 2, 0), rows - NA_BAND)
        per_j = []
        for j in range(NA_QROWS):
            i = i0 + j
            rs = min(max(i - NA_KH // 2, 0), rows - NA_KH)
            per_p = []
            for p in range(NA_BAND // 2):
                pair = tuple(kr - i + NA_KH - 1 if rs <= kr < rs + NA_KH else NA_ROW_OUT
                             for kr in (b0 + 2 * p, b0 + 2 * p + 1))
                per_p.append(combos.setdefault(pair, len(combos)))
            per_j.append(per_p)
        plan.append(per_j)
    return plan, list(combos)


def _na_bias_tables(rpb, pairs):
    col = jnp.arange(GRID_W)
    col_start = jnp.clip(col - NA_KW // 2, 0, GRID_W - NA_KW)
    col_mask = (col[None, :] >= col_start[:, None]) & (col[None, :] < col_start[:, None] + NA_KW)
    dc = jnp.clip(col[None, :] - col[:, None], -(NA_KW - 1), NA_KW - 1) + NA_KW - 1
    col_tab = jnp.einsum('hdc,qkc->hdqk', rpb, jax.nn.one_hot(dc, 2 * NA_KW - 1, dtype=F32),
                         precision=lax.Precision.HIGHEST)
    col_tab = jnp.where(col_mask[None, None], col_tab, NEG)
    out_tab = jnp.full_like(col_tab[:, :1], NEG)
    ext = jnp.concatenate([col_tab, out_tab], axis=1)
    return jnp.stack([jnp.concatenate([ext[:, a], ext[:, b]], axis=-1) for a, b in pairs], axis=1)


def _na_kernel(q_ref, k0, k1, k2, k3, v0, v1, v2, v3, bias_ref, qw_ref, kw_ref, o_ref, *, plan):
    i = pl.program_id(2)
    first = i == 0
    last = i == pl.num_programs(2) - 1

    def bias_tile(j, p):
        a, b, c = plan[0][j][p], plan[1][j][p], plan[2][j][p]
        idx = a if a == b == c else jnp.where(first, a, jnp.where(last, c, b))
        return bias_ref[0, idx]

    q = q_ref[...]
    qn = q * lax.rsqrt(jnp.mean(q * q, axis=-1, keepdims=True) + NORM_EPS) * qw_ref[...]
    qn = (qn * (NA_HEAD_DIM ** -0.5)).astype(BF16)
    logits = []
    for n, kr in enumerate((k0, k1, k2, k3)):
        k = kr[...]
        kn = (k * lax.rsqrt(jnp.mean(k * k, axis=-1, keepdims=True) + NORM_EPS) * kw_ref[...]).astype(BF16)
        bias = jnp.concatenate(
            [jnp.concatenate([bias_tile(j, 2 * n), bias_tile(j, 2 * n + 1)], axis=1) for j in range(NA_QROWS)],
            axis=0)
        logits.append(_dot_nt(qn, kn) + bias)
    m = logits[0].max(axis=-1, keepdims=True)
    for l in logits[1:]:
        m = jnp.maximum(m, l.max(axis=-1, keepdims=True))
    den = jnp.zeros_like(m)
    acc = jnp.zeros(o_ref.shape, F32)
    for l, vr in zip(logits, (v0, v1, v2, v3)):
        p = jnp.exp(l - m)
        den = den + p.sum(axis=-1, keepdims=True)
        acc = acc + _dot(p.astype(BF16), vr[...].astype(BF16))
    o_ref[...] = acc / den


def _na(z, rpb, qw, kw, B, S):
    T = z.shape[0]
    rows = S // GRID_W
    nblk = rows // NA_QROWS
    nkb = S // NA_KBLK
    plan, pairs = _na_pair_plan(rows)
    bias = _na_bias_tables(rpb, pairs)

    def kv_map(col0, n):
        def f(b, h, i):
            start = jnp.clip(2 * i - 1, 0, nkb - 4)
            return (b * nkb + start + n, col0 + h)
        return f

    q_spec = pl.BlockSpec((NA_TQ, LANE), lambda b, h, i: (b * nblk + i, h))
    k_specs = [pl.BlockSpec((NA_KBLK, LANE), kv_map(NA_HEADS, n)) for n in range(4)]
    v_specs = [pl.BlockSpec((NA_KBLK, LANE), kv_map(2 * NA_HEADS, n)) for n in range(4)]
    w_spec = pl.BlockSpec((1, LANE), lambda b, h, i: (0, 0))
    return pl.pallas_call(
        functools.partial(_na_kernel, plan=plan),
        out_shape=jax.ShapeDtypeStruct((T, NA_WIDTH), F32),
        grid=(B, NA_HEADS, nblk),
        in_specs=[q_spec, *k_specs, *v_specs,
                  pl.BlockSpec((1, len(pairs), GRID_W, 2 * GRID_W), lambda b, h, i: (h, 0, 0, 0)),
                  w_spec, w_spec],
        out_specs=pl.BlockSpec((NA_TQ, LANE), lambda b, h, i: (b * nblk + i, h)),
        compiler_params=_cparams(("arbitrary", "arbitrary", "arbitrary")),
        name="na_attn",
    )(z, z, z, z, z, z, z, z, z, bias, qw.reshape(1, LANE), kw.reshape(1, LANE))


def _rwkv_prep_kernel(r_ref, k_ref, v_ref, l_ref,
                      rp_ref, kp_ref, vp_ref, lp_ref,
                      rn_ref, kn_ref, vn_ref, ln_ref,
                      mu_ref, mul_ref, w0_ref, w2_ref, a0_ref, a2_ref, g2_ref, kk_ref, hs_ref,
                      ro_ref, ko_ref, vo_ref, kko_ref, iclr_ref, logw_ref, g_ref, *, tpb):
    i = pl.program_id(0)
    first = (i % tpb) == 0
    last = (i % tpb) == tpb - 1
    tm = r_ref.shape[0]
    rows = lax.broadcasted_iota(jnp.int32, (tm, 1), 0)

    def shift(y_ref, p_ref, n_ref, mu0, mu1):
        y = y_ref[...]
        prow = jnp.where(first, 0.0, p_ref[7:8, :])
        nrow = jnp.where(last, 0.0, n_ref[0:1, :])
        prev = jnp.where(rows == 0, prow, pltpu.roll(y, 1, 0))
        nxt = jnp.where(rows == tm - 1, nrow, pltpu.roll(y, tm - 1, 0))
        return y + mu0 * (prev - y) + mu1 * (nxt - y)

    r = shift(r_ref, rp_ref, rn_ref, mu_ref[0:1, 0:1024], mu_ref[1:2, 0:1024])
    k = shift(k_ref, kp_ref, kn_ref, mu_ref[0:1, 1024:2048], mu_ref[1:2, 1024:2048])
    v = shift(v_ref, vp_ref, vn_ref, mu_ref[0:1, 2048:3072], mu_ref[1:2, 2048:3072])
    lo = shift(l_ref, lp_ref, ln_ref, mul_ref[0:1, :], mul_ref[1:2, :])
    ro_ref[...] = r
    ko_ref[...] = k
    vo_ref[...] = v

    for d in range(2):
        wd = jnp.tanh(lo[:, d * LANE:(d + 1) * LANE]).astype(BF16)
        w_logit = w0_ref[d:d + 1, :] + _dot(wd, w2_ref[d])
        x = -w_logit
        softplus = jnp.maximum(x, 0.0) + jnp.log(1.0 + jnp.exp(-jnp.abs(x)))
        logw_ref[d] = -jnp.exp(-softplus - 0.5)
        ad = lo[:, (2 + d) * LANE:(3 + d) * LANE].astype(BF16)
        iclr_ref[d] = jax.nn.sigmoid(a0_ref[d:d + 1, :] + _dot(ad, a2_ref[d]))
    gd = jax.nn.sigmoid(lo[:, 4 * LANE:4 * LANE + GATE_LORA]).astype(BF16)
    g_ref[...] = _dot(gd, g2_ref[...])

    kk = k * kk_ref[...]
    ss = _head_sums(kk * kk, hs_ref[...])
    kko_ref[...] = kk / jnp.maximum(jnp.sqrt(ss), 1e-12)


def _rwkv_prep(z, mu_rkv, mu_lora, w0, w2p, a0, a2p, g2, k_k, headsum, S):
    T = z.shape[0]
    C = RWKV_WIDTH
    tm = 256
    tpb = S // tm
    nt = T // tm
    hb = tm // 8
    cb = COL_RKV // C
    lb = COL_LORA // LORA_W

    def main(col):
        return pl.BlockSpec((tm, C), lambda i: (i, col))

    def prev(col):
        return pl.BlockSpec((8, C), lambda i: (jnp.maximum(i * hb - 1, 0), col))

    def nxt(col):
        return pl.BlockSpec((8, C), lambda i: (jnp.minimum((i + 1) * hb, T // 8 - 1), col))

    def full(shape):
        nd = len(shape)
        return pl.BlockSpec(shape, lambda i: (0,) * nd)

    cols = (cb, cb + 1, cb + 2, lb)
    out_tc = jax.ShapeDtypeStruct((T, C), F32)
    out_2tc = jax.ShapeDtypeStruct((2, T, C), F32)
    spec_tc = pl.BlockSpec((tm, C), lambda i: (i, 0))
    spec_2tc = pl.BlockSpec((2, tm, C), lambda i: (0, i, 0))
    return pl.pallas_call(
        functools.partial(_rwkv_prep_kernel, tpb=tpb),
        out_shape=(out_tc, out_tc, out_tc, out_tc, out_2tc, out_2tc, out_tc),
        grid=(nt,),
        in_specs=[*[main(c) for c in cols], *[prev(c) for c in cols], *[nxt(c) for c in cols],
                  full(mu_rkv.shape), full(mu_lora.shape), full(w0.shape), full(w2p.shape),
                  full(a0.shape), full(a2p.shape), full(g2.shape), full((1, C)), full(headsum.shape)],
        out_specs=(spec_tc, spec_tc, spec_tc, spec_tc, spec_2tc, spec_2tc, spec_tc),
        compiler_params=_cparams(("arbitrary",)),
        name="rwkv_prep",
    )(z, z, z, z, z, z, z, z, z, z, z, z,
      mu_rkv, mu_lora, w0, w2p, a0, a2p, g2, k_k.reshape(1, C), headsum)


def _rwkv_scan_kernel(rf_ref, kf_ref, vf_ref, kkf_ref, rb_ref, kb_ref, vb_ref, kkb_ref,
                      iclrf_ref, iclrb_ref, logwf_ref, logwb_ref, ka_ref, of_ref, ob_ref, s_ref):
    L = RW_CHUNK
    n_pairs = RWKV_WIDTH // LANE

    @pl.when(pl.program_id(1) == 0)
    def _():
        s_ref[...] = jnp.zeros_like(s_ref)

    ti = lax.broadcasted_iota(jnp.int32, (L, L), 0)
    si = lax.broadcasted_iota(jnp.int32, (L, L), 1)
    R = lax.broadcasted_iota(jnp.int32, (2 * L, 2 * L), 0)
    Cc = lax.broadcasted_iota(jnp.int32, (2 * L, 2 * L), 1)
    same = (R >= L) == (Cc >= L)
    eye = jnp.where(R == Cc, 1.0, 0.0)
    lane = lax.broadcasted_iota(jnp.int32, (1, LANE), 1)
    head0 = lane < RWKV_HEAD_DIM
    ones_l = jnp.ones((L, LANE), BF16)

    def stack(x):
        return jnp.concatenate([jnp.where(head0, x, 0.0), jnp.where(head0, 0.0, x)], axis=0)

    dirs = ((rf_ref, kf_ref, vf_ref, kkf_ref, iclrf_ref, logwf_ref, of_ref, 1),
            (rb_ref, kb_ref, vb_ref, kkb_ref, iclrb_ref, logwb_ref, ob_ref, -1))
    chains = []
    for d, (r_ref, k_ref, v_ref, kk_ref, iclr_ref, logw_ref, o_ref, sgn) in enumerate(dirs):
        cum_mask = jnp.where((ti - si) * sgn >= 0, 1.0, 0.0).astype(BF16)
        m_strict = same & ((R - Cc) * sgn > 0)
        m_incl = same & ((R - Cc) * sgn >= 0)
        lw = logw_ref[0]
        lw_hi = lw.astype(BF16)
        lw_lo = (lw - lw_hi.astype(F32)).astype(BF16)
        cs = _dot(cum_mask, lw_hi) + _dot(cum_mask, lw_lo)
        tot = _dot_tn(lw_hi, ones_l) + _dot_tn(lw_lo, ones_l)
        e_in = jnp.exp(cs)
        e_out = jnp.exp(-cs)
        kk = kk_ref[...]
        iclr = iclr_ref[0]
        a_t = -kk * jnp.exp(cs - lw)
        b_t = kk * iclr * e_out
        k_t = k_ref[...] * (1.0 + (iclr - 1.0) * ka_ref[...]) * e_out
        r_t = r_ref[...] * e_in
        v = v_ref[...]
        for p in range(n_pairs):
            sl = slice(p * LANE, (p + 1) * LANE)
            chains.append(dict(
                d=d, p=p, o_ref=o_ref, sl=sl, m_strict=m_strict, m_incl=m_incl,
                vs=stack(v[:, sl]).astype(BF16),
                ar=jnp.concatenate([stack(a_t[:, sl]), stack(r_t[:, sl])], axis=0).astype(BF16),
                bk=jnp.concatenate([stack(b_t[:, sl]), stack(k_t[:, sl])], axis=0).astype(BF16),
                decay=jnp.exp(tot[sl, :])))

    for c in chains:
        aa = _dot_nt(c["ar"], c["bk"])
        c["a_ab"] = jnp.where(c["m_strict"], aa[:2 * L, :2 * L], 0.0)
        c["a_ak"] = jnp.where(c["m_strict"], aa[:2 * L, 2 * L:], 0.0).astype(BF16)
        c["a_r"] = jnp.concatenate([jnp.where(c["m_incl"], aa[2 * L:, :2 * L], 0.0),
                                    jnp.where(c["m_incl"], aa[2 * L:, 2 * L:], 0.0)], axis=1).astype(BF16)
        c["pw"] = c["a_ab"].astype(BF16)
        c["tinv"] = eye + c["a_ab"]
    for _ in range(5):
        for c in chains:
            c["pw2"] = _dot(c["pw"], c["pw"]).astype(BF16)
        for c in chains:
            c["tinv"] = c["tinv"] + _dot(c["tinv"].astype(BF16), c["pw2"])
            c["pw"] = c["pw2"]
    for c in chains:
        c["sp"] = s_ref[c["d"], c["p"]]
        c["sa_sr"] = _dot(c["ar"], c["sp"].astype(BF16))
        c["akv"] = _dot(c["a_ak"], c["vs"])
    for c in chains:
        x1 = c["sa_sr"][:2 * L] + c["akv"]
        u = _dot(c["tinv"].astype(BF16), x1.astype(BF16))
        c["uv"] = jnp.concatenate([u.astype(BF16), c["vs"]], axis=0)
    for c in chains:
        o_s = c["sa_sr"][2 * L:] + _dot(c["a_r"], c["uv"])
        c["o_ref"][:, c["sl"]] = o_s[:L] + o_s[L:]
        s_ref[c["d"], c["p"]] = (c["sp"] + _dot_tn(c["bk"], c["uv"])) * c["decay"]


def _rwkv_scan(r, k, v, kk, iclr, logw, k_a, B, S):
    T, C = r.shape
    L = RW_CHUNK
    nc = S // L
    fwd = pl.BlockSpec((L, C), lambda b, c: (b * nc + c, 0))
    bwd = pl.BlockSpec((L, C), lambda b, c: (b * nc + nc - 1 - c, 0))
    fwd2 = pl.BlockSpec((1, L, C), lambda b, c: (0, b * nc + c, 0))
    bwd2 = pl.BlockSpec((1, L, C), lambda b, c: (1, b * nc + nc - 1 - c, 0))
    out = jax.ShapeDtypeStruct((T, C), F32)
    return pl.pallas_call(
        _rwkv_scan_kernel,
        out_shape=(out, out),
        grid=(B, nc),
        in_specs=[fwd, fwd, fwd, fwd, bwd, bwd, bwd, bwd, fwd2, bwd2, fwd2, bwd2,
                  pl.BlockSpec((1, C), lambda b, c: (0, 0))],
        out_specs=(fwd, bwd),
        scratch_shapes=[pltpu.VMEM((2, C // LANE, LANE, LANE), F32)],
        compiler_params=_cparams(("arbitrary", "arbitrary")),
        name="rwkv_scan",
    )(r, k, v, kk, r, k, v, kk, iclr, iclr, logw, logw, k_a.reshape(1, C))


def _merge_kernel(of_ref, ob_ref, r_ref, k_ref, v_ref, g_ref, oa_ref, za_ref, zb_ref,
                  hs_ref, rk_ref, lnw_ref, lnb_ref, bga_ref, bgb_ref, wa_ref, wb_ref,
                  out_ref, yb_ref, oab_ref):
    @pl.when(pl.program_id(1) == 0)
    def _():
        hs = hs_ref[...]
        o = of_ref[...] + ob_ref[...]
        inv_n = 1.0 / RWKV_HEAD_DIM
        mean = _head_sums(o, hs) * inv_n
        cen = o - mean
        var = _head_sums(cen * cen, hs) * inv_n
        y = cen * lax.rsqrt(var + RWKV_GN_EPS) * lnw_ref[...] + lnb_ref[...]
        v = v_ref[...]
        bonus = _head_sums(r_ref[...] * k_ref[...] * rk_ref[...], hs) * v
        yb_ref[...] = ((y + bonus) * g_ref[...]).astype(BF16)
        oab_ref[...] = oa_ref[...].astype(BF16)

    ga = jax.nn.sigmoid(za_ref[...] + bga_ref[...])
    gb = jax.nn.sigmoid(zb_ref[...] + bgb_ref[...])
    out_ref[...] = ga * _dot(oab_ref[...], wa_ref[...]) + gb * _dot(yb_ref[...], wb_ref[...])


def _merge(o_f, o_b, r, k, v, g, oa, z, headsum, r_k, ln_w, ln_b, b_gate, wa, wb, D):
    T, C = r.shape
    tm, tn = 256, D
    nj = D // tn
    gcol = COL_GATE // tn
    tc = pl.BlockSpec((tm, C), lambda i, j: (i, 0))

    def vec(n):
        return pl.BlockSpec((1, n), lambda i, j: (0, 0))

    return pl.pallas_call(
        _merge_kernel,
        out_shape=jax.ShapeDtypeStruct((T, D), F32),
        grid=(T // tm, nj),
        in_specs=[tc, tc, tc, tc, tc, tc, tc,
                  pl.BlockSpec((tm, tn), lambda i, j: (i, gcol + j)),
                  pl.BlockSpec((tm, tn), lambda i, j: (i, gcol + nj + j)),
                  pl.BlockSpec((LANE, LANE), lambda i, j: (0, 0)), vec(C), vec(C), vec(C),
                  pl.BlockSpec((1, tn), lambda i, j: (0, j)),
                  pl.BlockSpec((1, tn), lambda i, j: (0, nj + j)),
                  pl.BlockSpec((C, tn), lambda i, j: (0, j)),
                  pl.BlockSpec((C, tn), lambda i, j: (0, j))],
        out_specs=pl.BlockSpec((tm, tn), lambda i, j: (i, j)),
        scratch_shapes=[pltpu.VMEM((tm, C), BF16), pltpu.VMEM((tm, C), BF16)],
        compiler_params=_cparams(("arbitrary", "arbitrary")),
        name="merge",
    )(o_f, o_b, r, k, v, g, oa, z, z, headsum, r_k.reshape(1, C), ln_w.reshape(1, C), ln_b.reshape(1, C),
      b_gate.reshape(1, 2 * D), b_gate.reshape(1, 2 * D), wa, wb)


def _outproj_kernel(m_ref, x_ref, mod_ref, nw_ref, w_ref, x1_ref, u2_ref):
    x1 = x_ref[...] + mod_ref[0, 2:3, :] * _dot(m_ref[...].astype(BF16), w_ref[...])
    x1_ref[...] = x1
    ms = jnp.mean(x1 * x1, axis=-1, keepdims=True)
    y = x1 * lax.rsqrt(ms + NORM_EPS) * nw_ref[...]
    u2_ref[...] = (y * (1.0 + mod_ref[0, 4:5, :]) + mod_ref[0, 3:4, :]).astype(BF16)


def _outproj(merged, x2, mod, norm_w, w_out, S):
    T, D = x2.shape
    tm = 256
    tpb = S // tm
    td = pl.BlockSpec((tm, D), lambda i: (i, 0))
    return pl.pallas_call(
        _outproj_kernel,
        out_shape=(jax.ShapeDtypeStruct((T, D), F32), jax.ShapeDtypeStruct((T, D), BF16)),
        grid=(T // tm,),
        in_specs=[td, td, pl.BlockSpec((1, 6, D), lambda i: (i // tpb, 0, 0)),
                  pl.BlockSpec((1, D), lambda i: (0, 0)), pl.BlockSpec((D, D), lambda i: (0, 0))],
        out_specs=(td, td),
        compiler_params=_cparams(("arbitrary",)),
        name="outproj",
    )(merged, x2, mod, norm_w.reshape(1, D), w_out)


PEER_NV = PEER_TOPK + 1
PEER_VPAD = 24


def _peer_topk_kernel(u_ref, wq_ref, k1_ref, k2_ref, e1_ref, th_ref, e2_ref, vals_ref):
    tm = u_ref.shape[0]
    half = PEER_DQ // 2
    q = _dot(u_ref[...], wq_ref[...])
    vals_ref[...] = jnp.full(vals_ref.shape, NEG, F32)
    for h in range(PEER_HEADS):
        q1 = q[:, h * PEER_DQ:h * PEER_DQ + half].astype(BF16)
        q2 = q[:, h * PEER_DQ + half:(h + 1) * PEER_DQ].astype(BF16)
        s1 = _dot_nt(k1_ref[...], q1)
        s2 = _dot_nt(k2_ref[...], q2)
        for side, s in enumerate((s1, s2)):
            for it in range(PEER_NV):
                m = s.max(axis=0, keepdims=True)
                vals_ref[side, it:it + 1, :] = m
                s = jnp.where(s >= m, NEG, s)
        v1 = vals_ref[0]
        v2 = vals_ref[1]
        cands = [v1[a:a + 1, :] + v2[0:8, :] for a in range(8)]
        cands.append(v1[0:1, :] + v2[8:PEER_VPAD, :])
        cands.append(v1[8:PEER_VPAD, :] + v2[0:1, :])
        cand = jnp.concatenate(cands, axis=0)
        crow = lax.broadcasted_iota(jnp.int32, cand.shape, 0)
        tops = []
        for it in range(PEER_NV):
            m = cand.max(axis=0, keepdims=True)
            tops.append(m)
            first = jnp.where(cand >= m, crow, cand.shape[0]).min(axis=0, keepdims=True)
            cand = jnp.where(crow == first, NEG, cand)
        zsum = jnp.zeros((1, tm), F32)
        for t in tops[:PEER_TOPK]:
            zsum = zsum + jnp.exp(t - tops[0])
        tau = 0.5 * (tops[PEER_TOPK - 1] + tops[PEER_TOPK])
        m1 = v1[0:1, :]
        m2 = v2[0:1, :]
        e1_ref[h] = jnp.exp(s1 - m1) * (0.5 / zsum)
        e2_ref[h] = jnp.exp(s2 - m2)
        th_ref[h] = jnp.exp(jnp.minimum(tau - m2 - s1, 80.0))


def _peer_topk(u2, w_q, k1, k2):
    T, D = u2.shape
    tm = 512
    out = jax.ShapeDtypeStruct((PEER_HEADS, PEER_N_KEYS, T), F32)
    ospec = pl.BlockSpec((PEER_HEADS, PEER_N_KEYS, tm), lambda i: (0, 0, i))
    return pl.pallas_call(
        _peer_topk_kernel,
        out_shape=(out, out, out),
        grid=(T // tm,),
        in_specs=[pl.BlockSpec((tm, D), lambda i: (i, 0)),
                  pl.BlockSpec(w_q.shape, lambda i: (0, 0)),
                  pl.BlockSpec(k1.shape, lambda i: (0, 0)),
                  pl.BlockSpec(k2.shape, lambda i: (0, 0))],
        out_specs=(ospec, ospec, ospec),
        scratch_shapes=[pltpu.VMEM((2, PEER_VPAD, tm), F32)],
        compiler_params=_cparams(("arbitrary",)),
        name="peer_topk",
    )(u2, w_q, k1, k2)


PEER_TE = 1024


def _peer_mix_kernel(u_ref, x1_ref, mod_ref, e1_ref, th_ref, e2_ref, eu_ref, ev_ref, o_ref,
                     h_ref, hid_ref, acc_ref):
    j = pl.program_id(1)
    tm = u_ref.shape[0]

    @pl.when(j == 0)
    def _():
        acc_ref[...] = jnp.zeros_like(acc_ref)

    h_ref[...] = _dot_nt(eu_ref[...], u_ref[...])
    for ii in range(PEER_TE // PEER_N_KEYS):
        rs = slice(ii * PEER_N_KEYS, (ii + 1) * PEER_N_KEYS)
        for tc in range(tm // LANE):
            cs = slice(tc * LANE, (tc + 1) * LANE)
            gate = jnp.zeros((PEER_N_KEYS, LANE), F32)
            for h in range(PEER_HEADS):
                e2 = e2_ref[h, :, cs]
                sel = jnp.where(e2 >= th_ref[h, ii:ii + 1, cs], e2, 0.0)
                gate = gate + sel * e1_ref[h, ii:ii + 1, cs]
            hb = h_ref[rs, cs]
            act = hb * (1.0 + lax.erf(hb * (2.0 ** -0.5)))
            hid_ref[rs, cs] = (act * gate).astype(BF16)
    acc_ref[...] += _dot_tn(hid_ref[...], ev_ref[...])

    @pl.when(j == pl.num_programs(1) - 1)
    def _():
        o_ref[...] = x1_ref[...] + mod_ref[0, 5:6, :] * acc_ref[...]


def _peer_mix(u2, x1, mod, e1, th, e2, exp_u, exp_v, S):
    T, D = u2.shape
    tm = 512
    te = PEER_TE
    n1 = te // PEER_N_KEYS
    tpb = S // tm
    ne = exp_u.shape[0] // te
    td = pl.BlockSpec((tm, D), lambda i, j: (i, 0))
    return pl.pallas_call(
        _peer_mix_kernel,
        out_shape=jax.ShapeDtypeStruct((T, D), F32),
        grid=(T // tm, ne),
        in_specs=[td, td, pl.BlockSpec((1, 6, D), lambda i, j: (i // tpb, 0, 0)),
                  pl.BlockSpec((PEER_HEADS, n1, tm), lambda i, j: (0, j, i)),
                  pl.BlockSpec((PEER_HEADS, n1, tm), lambda i, j: (0, j, i)),
                  pl.BlockSpec((PEER_HEADS, PEER_N_KEYS, tm), lambda i, j: (0, 0, i)),
                  pl.BlockSpec((te, D), lambda i, j: (j, 0)),
                  pl.BlockSpec((te, D), lambda i, j: (j, 0))],
        out_specs=td,
        scratch_shapes=[pltpu.VMEM((te, tm), F32), pltpu.VMEM((te, tm), BF16), pltpu.VMEM((tm, D), F32)],
        compiler_params=_cparams(("arbitrary", "arbitrary")),
        name="peer_mix",
    )(u2, x1, mod, e1, th, e2, exp_u, exp_v)


def _pad_cols(w, n):
    return jnp.pad(w, ((0, 0), (0, n - w.shape[1])))


def _pack_in_weights(w_in):
    na = 3 * NA_WIDTH
    rk = 3 * RWKV_WIDTH
    za, zb, zg = w_in[:, :na], w_in[:, na:na + rk + 640], w_in[:, na + rk + 640:]
    lora = zb[:, rk:]
    parts, start = [], 0
    for s in (DECAY_LORA, DECAY_LORA, AAA_LORA, AAA_LORA):
        parts.append(_pad_cols(lora[:, start:start + s], LANE))
        start += s
    parts.append(lora[:, start:start + GATE_LORA])
    lora_p = _pad_cols(jnp.concatenate(parts, axis=1), LORA_W)
    return jnp.concatenate([za, zb[:, :rk], zg, lora_p], axis=1).astype(BF16)


def _pack_mu_lora(mu):
    lora = mu[:, 3 * RWKV_WIDTH:]
    parts, start = [], 0
    for s in (DECAY_LORA, DECAY_LORA, AAA_LORA, AAA_LORA):
        parts.append(_pad_cols(lora[:, start:start + s], LANE))
        start += s
    parts.append(lora[:, start:start + GATE_LORA])
    return _pad_cols(jnp.concatenate(parts, axis=1), LORA_W)


def _block(x, c, ada_w, ada_b, norm1_w, norm2_w, w_in, b_gate, na_q_norm_w, na_k_norm_w, na_rpb,
           rwkv_mu, rwkv_w0, rwkv_w2, rwkv_a0, rwkv_a2, rwkv_g2, rwkv_k_k, rwkv_k_a, rwkv_r_k,
           rwkv_ln_w, rwkv_ln_b, w_branch_a, w_branch_b, w_out, peer_w_q, peer_sub_k1, peer_sub_k2,
           peer_u, peer_v):
    B, S, D = x.shape
    T = B * S
    x2 = x.reshape(T, D)
    mod = _adaln(c, ada_w, ada_b)

    z = _inproj(x2, mod, norm1_w, _pack_in_weights(w_in), S)

    o_a = _na(z, na_rpb, na_q_norm_w, na_k_norm_w, B, S)

    head = jnp.arange(LANE) // RWKV_HEAD_DIM
    headsum = (head[:, None] == head[None, :]).astype(BF16)
    w2p = jnp.pad(rwkv_w2, ((0, 0), (0, LANE - DECAY_LORA), (0, 0))).astype(BF16)
    a2p = jnp.pad(rwkv_a2, ((0, 0), (0, LANE - AAA_LORA), (0, 0))).astype(BF16)
    r, k, v, kk, iclr, logw, g = _rwkv_prep(
        z, rwkv_mu[:, :3 * RWKV_WIDTH], _pack_mu_lora(rwkv_mu), rwkv_w0, w2p, rwkv_a0, a2p,
        rwkv_g2.astype(BF16), rwkv_k_k, headsum, S)
    o_f, o_b = _rwkv_scan(r, k, v, kk, iclr, logw, rwkv_k_a, B, S)

    merged = _merge(o_f, o_b, r, k, v, g, o_a, z, headsum, rwkv_r_k.reshape(-1), rwkv_ln_w, rwkv_ln_b,
                    b_gate, w_branch_a.astype(BF16), w_branch_b.astype(BF16), D)
    x1, u2 = _outproj(merged, x2, mod, norm2_w, w_out.astype(BF16), S)

    e1, th, e2 = _peer_topk(u2, peer_w_q.astype(BF16), peer_sub_k1.astype(BF16), peer_sub_k2.astype(BF16))
    out = _peer_mix(u2, x1, mod, e1, th, e2, peer_u.astype(BF16), peer_v.astype(BF16), S)
    return out.reshape(B, S, D)


def kernel(x, c, ada_w, ada_b, norm1_w, norm2_w, w_in, b_gate, na_q_norm_w, na_k_norm_w, na_rpb, rwkv_mu, rwkv_w0, rwkv_w2, rwkv_a0, rwkv_a2, rwkv_g2, rwkv_k_k, rwkv_k_a, rwkv_r_k, rwkv_ln_w, rwkv_ln_b, w_branch_a, w_branch_b, w_out, peer_w_q, peer_sub_k1, peer_sub_k2, peer_u, peer_v):
    h = x
    for layer in range(ada_w.shape[0]):
        h = _block(h, c, ada_w[layer], ada_b[layer], norm1_w[layer], norm2_w[layer], w_in[layer],
                   b_gate[layer], na_q_norm_w[layer], na_k_norm_w[layer], na_rpb[layer],
                   rwkv_mu[layer], rwkv_w0[layer], rwkv_w2[layer], rwkv_a0[layer], rwkv_a2[layer],
                   rwkv_g2[layer], rwkv_k_k[layer], rwkv_k_a[layer], rwkv_r_k[layer],
                   rwkv_ln_w[layer], rwkv_ln_b[layer], w_branch_a[layer], w_branch_b[layer],
                   w_out[layer], peer_w_q[layer], peer_sub_k1[layer], peer_sub_k2[layer],
                   peer_u[layer], peer_v[layer])
    return h
```

```python
import functools

import jax
import jax.numpy as jnp
from jax import lax
from jax.experimental import pallas as pl
from jax.experimental.pallas import tpu as pltpu

F32 = jnp.float32
BF16 = jnp.bfloat16

GRID_W = 64
NA_HEADS = 8
NA_HEAD_DIM = 128
NA_WIDTH = NA_HEADS * NA_HEAD_DIM
NA_KH = 8
NA_KW = 16
RWKV_HEAD_DIM = 64
RWKV_HEADS = 16
RWKV_WIDTH = RWKV_HEADS * RWKV_HEAD_DIM
DECAY_LORA = 96
AAA_LORA = 96
GATE_LORA = 256
RWKV_GN_EPS = 64e-5
PEER_HEADS = 8
PEER_DQ = 256
PEER_N_KEYS = 128
PEER_TOPK = 16
NORM_EPS = 1e-6

LANE = 128
VMEM_LIMIT = 56 * 1024 * 1024
NEG = -1e30

COL_QKV = 0
COL_RKV = 3 * NA_WIDTH
COL_GATE = COL_RKV + 3 * RWKV_WIDTH
COL_LORA = COL_GATE + 4096
IN_PAD = COL_LORA + 1024
LORA_W = 1024

RW_CHUNK = 64


def _cparams(sem):
    return pltpu.CompilerParams(dimension_semantics=sem, vmem_limit_bytes=VMEM_LIMIT)


def _dot(a, b):
    return jnp.dot(a, b, preferred_element_type=F32)


def _dot_nt(a, b):
    return lax.dot_general(a, b, (((1,), (1,)), ((), ())), preferred_element_type=F32)


def _dot_tn(a, b):
    return lax.dot_general(a, b, (((0,), (0,)), ((), ())), preferred_element_type=F32)


def _head_sums(x, hs):
    hi = x.astype(BF16)
    lo = (x - hi.astype(F32)).astype(BF16)
    outs = []
    for g in range(x.shape[1] // LANE):
        sl = slice(g * LANE, (g + 1) * LANE)
        outs.append(_dot(hi[:, sl], hs) + _dot(lo[:, sl], hs))
    return jnp.concatenate(outs, axis=1)


def _adaln_kernel(c_ref, w_ref, b_ref, o_ref):
    c = c_ref[...]
    s = c * jax.nn.sigmoid(c)
    o_ref[...] = jnp.dot(s, w_ref[...], preferred_element_type=F32,
                         precision=lax.Precision.HIGHEST) + b_ref[...]


def _adaln(c, ada_w, ada_b):
    B, D = c.shape
    n = ada_w.shape[1]
    tn = 1024
    cp = jnp.zeros((8, D), F32).at[:B].set(c)
    out = pl.pallas_call(
        _adaln_kernel,
        out_shape=jax.ShapeDtypeStruct((8, n), F32),
        grid=(n // tn,),
        in_specs=[pl.BlockSpec((8, D), lambda j: (0, 0)),
                  pl.BlockSpec((D, tn), lambda j: (0, j)),
                  pl.BlockSpec((1, tn), lambda j: (0, j))],
        out_specs=pl.BlockSpec((8, tn), lambda j: (0, j)),
        compiler_params=_cparams(("arbitrary",)),
        name="adaln",
    )(cp, ada_w, ada_b.reshape(1, n))
    return out[:B].reshape(B, 6, D)


def _inproj_kernel(x_ref, mod_ref, nw_ref, w_ref, o_ref, u_ref):
    @pl.when(pl.program_id(1) == 0)
    def _():
        x = x_ref[...]
        ms = jnp.mean(x * x, axis=-1, keepdims=True)
        y = x * lax.rsqrt(ms + NORM_EPS) * nw_ref[...]
        u = y * (1.0 + mod_ref[0, 1:2, :]) + mod_ref[0, 0:1, :]
        u_ref[...] = u.astype(BF16)

    o_ref[...] = _dot(u_ref[...], w_ref[...])


def _inproj(x2, mod, norm_w, w_pack, S):
    T, D = x2.shape
    n = w_pack.shape[1]
    tm = min(1024, S)
    tn = 1024
    tpb = S // tm
    return pl.pallas_call(
        _inproj_kernel,
        out_shape=jax.ShapeDtypeStruct((T, n), F32),
        grid=(T // tm, n // tn),
        in_specs=[pl.BlockSpec((tm, D), lambda i, j: (i, 0)),
                  pl.BlockSpec((1, 6, D), lambda i, j: (i // tpb, 0, 0)),
                  pl.BlockSpec((1, D), lambda i, j: (0, 0)),
                  pl.BlockSpec((D, tn), lambda i, j: (0, j))],
        out_specs=pl.BlockSpec((tm, tn), lambda i, j: (i, j)),
        scratch_shapes=[pltpu.VMEM((tm, D), BF16)],
        compiler_params=_cparams(("arbitrary", "arbitrary")),
        name="inproj",
    )(x2, mod, norm_w.reshape(1, D), w_pack)


NA_QROWS = 8
NA_BAND = 16
NA_TQ = NA_QROWS * GRID_W
NA_KBLK = 4 * GRID_W


NA_ROW_OUT = 2 * NA_KH - 1


def _na_pair_plan(rows):
    combos, plan = {}, []
    for i0 in (0, min(NA_QROWS, rows - NA_QROWS), rows - NA_QROWS):
        b0 = min(max(i0 - NA_KH // 2, 0), rows - NA_BAND)
        per_j = []
        for j in range(NA_QROWS):
            i = i0 + j
            rs = min(max(i - NA_KH // 2, 0), rows - NA_KH)
            per_p = []
            for p in range(NA_BAND // 2):
                pair = tuple(kr - i + NA_KH - 1 if rs <= kr < rs + NA_KH else NA_ROW_OUT
                             for kr in (b0 + 2 * p, b0 + 2 * p + 1))
                per_p.append(combos.setdefault(pair, len(combos)))
            per_j.append(per_p)
        plan.append(per_j)
    return plan, list(combos)


def _na_bias_tables(rpb, pairs):
    col = jnp.arange(GRID_W)
    col_start = jnp.clip(col - NA_KW // 2, 0, GRID_W - NA_KW)
    col_mask = (col[None, :] >= col_start[:, None]) & (col[None, :] < col_start[:, None] + NA_KW)
    dc = jnp.clip(col[None, :] - col[:, None], -(NA_KW - 1), NA_KW - 1) + NA_KW - 1
    col_tab = jnp.einsum('hdc,qkc->hdqk', rpb, jax.nn.one_hot(dc, 2 * NA_KW - 1, dtype=F32),
                         precision=lax.Precision.HIGHEST)
    col_tab = jnp.where(col_mask[None, None], col_tab, NEG)
    out_tab = jnp.full_like(col_tab[:, :1], NEG)
    ext = jnp.concatenate([col_tab, out_tab], axis=1)
    return jnp.stack([jnp.concatenate([ext[:, a], ext[:, b]], axis=-1) for a, b in pairs], axis=1)


def _na_kernel(q_ref, k0, k1, k2, k3, v0, v1, v2, v3, bias_ref, qw_ref, kw_ref, o_ref, *, plan):
    i = pl.program_id(2)
    first = i == 0
    last = i == pl.num_programs(2) - 1

    def bias_tile(j, p):
        a, b, c = plan[0][j][p], plan[1][j][p], plan[2][j][p]
        idx = a if a == b == c else jnp.where(first, a, jnp.where(last, c, b))
        return bias_ref[0, idx]

    q = q_ref[...]
    qn = q * lax.rsqrt(jnp.mean(q * q, axis=-1, keepdims=True) + NORM_EPS) * qw_ref[...]
    qn = (qn * (NA_HEAD_DIM ** -0.5)).astype(BF16)
    logits = []
    for n, kr in enumerate((k0, k1, k2, k3)):
        k = kr[...]
        kn = (k * lax.rsqrt(jnp.mean(k * k, axis=-1, keepdims=True) + NORM_EPS) * kw_ref[...]).astype(BF16)
        bias = jnp.concatenate(
            [jnp.concatenate([bias_tile(j, 2 * n), bias_tile(j, 2 * n + 1)], axis=1) for j in range(NA_QROWS)],
            axis=0)
        logits.append(_dot_nt(qn, kn) + bias)
    m = logits[0].max(axis=-1, keepdims=True)
    for l in logits[1:]:
        m = jnp.maximum(m, l.max(axis=-1, keepdims=True))
    den = jnp.zeros_like(m)
    acc = jnp.zeros(o_ref.shape, F32)
    for l, vr in zip(logits, (v0, v1, v2, v3)):
        p = jnp.exp(l - m)
        den = den + p.sum(axis=-1, keepdims=True)
        acc = acc + _dot(p.astype(BF16), vr[...].astype(BF16))
    o_ref[...] = acc / den


def _na(z, rpb, qw, kw, B, S):
    T = z.shape[0]
    rows = S // GRID_W
    nblk = rows // NA_QROWS
    nkb = S // NA_KBLK
    plan, pairs = _na_pair_plan(rows)
    bias = _na_bias_tables(rpb, pairs)

    def kv_map(col0, n):
        def f(b, h, i):
            start = jnp.clip(2 * i - 1, 0, nkb - 4)
            return (b * nkb + start + n, col0 + h)
        return f

    q_spec = pl.BlockSpec((NA_TQ, LANE), lambda b, h, i: (b * nblk + i, h))
    k_specs = [pl.BlockSpec((NA_KBLK, LANE), kv_map(NA_HEADS, n)) for n in range(4)]
    v_specs = [pl.BlockSpec((NA_KBLK, LANE), kv_map(2 * NA_HEADS, n)) for n in range(4)]
    w_spec = pl.BlockSpec((1, LANE), lambda b, h, i: (0, 0))
    return pl.pallas_call(
        functools.partial(_na_kernel, plan=plan),
        out_shape=jax.ShapeDtypeStruct((T, NA_WIDTH), F32),
        grid=(B, NA_HEADS, nblk),
        in_specs=[q_spec, *k_specs, *v_specs,
                  pl.BlockSpec((1, len(pairs), GRID_W, 2 * GRID_W), lambda b, h, i: (h, 0, 0, 0)),
                  w_spec, w_spec],
        out_specs=pl.BlockSpec((NA_TQ, LANE), lambda b, h, i: (b * nblk + i, h)),
        compiler_params=_cparams(("arbitrary", "arbitrary", "arbitrary")),
        name="na_attn",
    )(z, z, z, z, z, z, z, z, z, bias, qw.reshape(1, LANE), kw.reshape(1, LANE))


def _rwkv_prep_kernel(r_ref, k_ref, v_ref, l_ref,
                      rp_ref, kp_ref, vp_ref, lp_ref,
                      rn_ref, kn_ref, vn_ref, ln_ref,
                      mu_ref, mul_ref, w0_ref, w2_ref, a0_ref, a2_ref, g2_ref, kk_ref, hs_ref,
                      ro_ref, ko_ref, vo_ref, kko_ref, iclr_ref, logw_ref, g_ref, *, tpb):
    i = pl.program_id(0)
    first = (i % tpb) == 0
    last = (i % tpb) == tpb - 1
    tm = r_ref.shape[0]
    rows = lax.broadcasted_iota(jnp.int32, (tm, 1), 0)

    def shift(y_ref, p_ref, n_ref, mu0, mu1):
        y = y_ref[...]
        prow = jnp.where(first, 0.0, p_ref[7:8, :])
        nrow = jnp.where(last, 0.0, n_ref[0:1, :])
        prev = jnp.where(rows == 0, prow, pltpu.roll(y, 1, 0))
        nxt = jnp.where(rows == tm - 1, nrow, pltpu.roll(y, tm - 1, 0))
        return y + mu0 * (prev - y) + mu1 * (nxt - y)

    r = shift(r_ref, rp_ref, rn_ref, mu_ref[0:1, 0:1024], mu_ref[1:2, 0:1024])
    k = shift(k_ref, kp_ref, kn_ref, mu_ref[0:1, 1024:2048], mu_ref[1:2, 1024:2048])
    v = shift(v_ref, vp_ref, vn_ref, mu_ref[0:1, 2048:3072], mu_ref[1:2, 2048:3072])
    lo = shift(l_ref, lp_ref, ln_ref, mul_ref[0:1, :], mul_ref[1:2, :])
    ro_ref[...] = r
    ko_ref[...] = k
    vo_ref[...] = v

    for d in range(2):
        wd = jnp.tanh(lo[:, d * LANE:(d + 1) * LANE]).astype(BF16)
        w_logit = w0_ref[d:d + 1, :] + _dot(wd, w2_ref[d])
        x = -w_logit
        softplus = jnp.maximum(x, 0.0) + jnp.log(1.0 + jnp.exp(-jnp.abs(x)))
        logw_ref[d] = -jnp.exp(-softplus - 0.5)
        ad = lo[:, (2 + d) * LANE:(3 + d) * LANE].astype(BF16)
        iclr_ref[d] = jax.nn.sigmoid(a0_ref[d:d + 1, :] + _dot(ad, a2_ref[d]))
    gd = jax.nn.sigmoid(lo[:, 4 * LANE:4 * LANE + GATE_LORA]).astype(BF16)
    g_ref[...] = _dot(gd, g2_ref[...])

    kk = k * kk_ref[...]
    ss = _head_sums(kk * kk, hs_ref[...])
    kko_ref[...] = kk / jnp.maximum(jnp.sqrt(ss), 1e-12)


def _rwkv_prep(z, mu_rkv, mu_lora, w0, w2p, a0, a2p, g2, k_k, headsum, S):
    T = z.shape[0]
    C = RWKV_WIDTH
    tm = 256
    tpb = S // tm
    nt = T // tm
    hb = tm // 8
    cb = COL_RKV // C
    lb = COL_LORA // LORA_W

    def main(col):
        return pl.BlockSpec((tm, C), lambda i: (i, col))

    def prev(col):
        return pl.BlockSpec((8, C), lambda i: (jnp.maximum(i * hb - 1, 0), col))

    def nxt(col):
        return pl.BlockSpec((8, C), lambda i: (jnp.minimum((i + 1) * hb, T // 8 - 1), col))

    def full(shape):
        nd = len(shape)
        return pl.BlockSpec(shape, lambda i: (0,) * nd)

    cols = (cb, cb + 1, cb + 2, lb)
    out_tc = jax.ShapeDtypeStruct((T, C), F32)
    out_2tc = jax.ShapeDtypeStruct((2, T, C), F32)
    spec_tc = pl.BlockSpec((tm, C), lambda i: (i, 0))
    spec_2tc = pl.BlockSpec((2, tm, C), lambda i: (0, i, 0))
    return pl.pallas_call(
        functools.partial(_rwkv_prep_kernel, tpb=tpb),
        out_shape=(out_tc, out_tc, out_tc, out_tc, out_2tc, out_2tc, out_tc),
        grid=(nt,),
        in_specs=[*[main(c) for c in cols], *[prev(c) for c in cols], *[nxt(c) for c in cols],
                  full(mu_rkv.shape), full(mu_lora.shape), full(w0.shape), full(w2p.shape),
                  full(a0.shape), full(a2p.shape), full(g2.shape), full((1, C)), full(headsum.shape)],
        out_specs=(spec_tc, spec_tc, spec_tc, spec_tc, spec_2tc, spec_2tc, spec_tc),
        compiler_params=_cparams(("arbitrary",)),
        name="rwkv_prep",
    )(z, z, z, z, z, z, z, z, z, z, z, z,
      mu_rkv, mu_lora, w0, w2p, a0, a2p, g2, k_k.reshape(1, C), headsum)


def _rwkv_scan_kernel(rf_ref, kf_ref, vf_ref, kkf_ref, rb_ref, kb_ref, vb_ref, kkb_ref,
                      iclrf_ref, iclrb_ref, logwf_ref, logwb_ref, ka_ref, of_ref, ob_ref, s_ref):
    L = RW_CHUNK
    n_pairs = RWKV_WIDTH // LANE

    @pl.when(pl.program_id(1) == 0)
    def _():
        s_ref[...] = jnp.zeros_like(s_ref)

    ti = lax.broadcasted_iota(jnp.int32, (L, L), 0)
    si = lax.broadcasted_iota(jnp.int32, (L, L), 1)
    R = lax.broadcasted_iota(jnp.int32, (2 * L, 2 * L), 0)
    Cc = lax.broadcasted_iota(jnp.int32, (2 * L, 2 * L), 1)
    same = (R >= L) == (Cc >= L)
    eye = jnp.where(R == Cc, 1.0, 0.0)
    lane = lax.broadcasted_iota(jnp.int32, (1, LANE), 1)
    head0 = lane < RWKV_HEAD_DIM

    def stack(x):
        return jnp.concatenate([jnp.where(head0, x, 0.0), jnp.where(head0, 0.0, x)], axis=0)

    dirs = ((rf_ref, kf_ref, vf_ref, kkf_ref, iclrf_ref, logwf_ref, of_ref, 1),
            (rb_ref, kb_ref, vb_ref, kkb_ref, iclrb_ref, logwb_ref, ob_ref, -1))
    chains = []
    for d, (r_ref, k_ref, v_ref, kk_ref, iclr_ref, logw_ref, o_ref, sgn) in enumerate(dirs):
        cum_mask = jnp.where((ti - si) * sgn >= 0, 1.0, 0.0).astype(BF16)
        m_strict = same & ((R - Cc) * sgn > 0)
        m_incl = same & ((R - Cc) * sgn >= 0)
        lw = logw_ref[0]
        lw_hi = lw.astype(BF16)
        lw_lo = (lw - lw_hi.astype(F32)).astype(BF16)
        cs = _dot(cum_mask, lw_hi) + _dot(cum_mask, lw_lo)
        tot = cs[L - 1:L, :] if sgn > 0 else cs[0:1, :]
        e_in = jnp.exp(cs)
        e_out = jnp.exp(-cs)
        kk = kk_ref[...]
        iclr = iclr_ref[0]
        a_t = -kk * jnp.exp(cs - lw)
        b_t = kk * iclr * e_out
        k_t = k_ref[...] * (1.0 + (iclr - 1.0) * ka_ref[...]) * e_out
        r_t = r_ref[...] * e_in
        v = v_ref[...]
        for p in range(n_pairs):
            sl = slice(p * LANE, (p + 1) * LANE)
            chains.append(dict(
                d=d, p=p, o_ref=o_ref, sl=sl, m_strict=m_strict, m_incl=m_incl,
                vs=stack(v[:, sl]).astype(BF16),
                ar=jnp.concatenate([stack(a_t[:, sl]), stack(r_t[:, sl])], axis=0).astype(BF16),
                bk=jnp.concatenate([stack(b_t[:, sl]), stack(k_t[:, sl])], axis=0).astype(BF16),
                decay=jnp.exp(tot[:, sl])))

    for c in chains:
        aa = _dot_nt(c["ar"], c["bk"])
        c["a_ab"] = jnp.where(c["m_strict"], aa[:2 * L, :2 * L], 0.0)
        c["a_ak"] = jnp.where(c["m_strict"], aa[:2 * L, 2 * L:], 0.0).astype(BF16)
        c["a_r"] = jnp.concatenate([jnp.where(c["m_incl"], aa[2 * L:, :2 * L], 0.0),
                                    jnp.where(c["m_incl"], aa[2 * L:, 2 * L:], 0.0)], axis=1).astype(BF16)
        c["pw"] = c["a_ab"].astype(BF16)
        c["tinv"] = eye + c["a_ab"]
    for _ in range(5):
        for c in chains:
            c["pw2"] = _dot(c["pw"], c["pw"]).astype(BF16)
        for c in chains:
            c["tinv"] = c["tinv"] + _dot(c["tinv"].astype(BF16), c["pw2"])
            c["pw"] = c["pw2"]
    for c in chains:
        c["sp"] = s_ref[c["d"], c["p"]]
        c["sa_sr"] = _dot_nt(c["ar"], c["sp"].astype(BF16))
        c["akv"] = _dot(c["a_ak"], c["vs"])
    for c in chains:
        x1 = c["sa_sr"][:2 * L] + c["akv"]
        u = _dot(c["tinv"].astype(BF16), x1.astype(BF16))
        c["uv"] = jnp.concatenate([u.astype(BF16), c["vs"]], axis=0)
    for c in chains:
        o_s = c["sa_sr"][2 * L:] + _dot(c["a_r"], c["uv"])
        c["o_ref"][:, c["sl"]] = o_s[:L] + o_s[L:]
        s_ref[c["d"], c["p"]] = (c["sp"] + _dot_tn(c["uv"], c["bk"])) * c["decay"]


def _rwkv_scan(r, k, v, kk, iclr, logw, k_a, B, S):
    T, C = r.shape
    L = RW_CHUNK
    nc = S // L
    fwd = pl.BlockSpec((L, C), lambda b, c: (b * nc + c, 0))
    bwd = pl.BlockSpec((L, C), lambda b, c: (b * nc + nc - 1 - c, 0))
    fwd2 = pl.BlockSpec((1, L, C), lambda b, c: (0, b * nc + c, 0))
    bwd2 = pl.BlockSpec((1, L, C), lambda b, c: (1, b * nc + nc - 1 - c, 0))
    out = jax.ShapeDtypeStruct((T, C), F32)
    return pl.pallas_call(
        _rwkv_scan_kernel,
        out_shape=(out, out),
        grid=(B, nc),
        in_specs=[fwd, fwd, fwd, fwd, bwd, bwd, bwd, bwd, fwd2, bwd2, fwd2, bwd2,
                  pl.BlockSpec((1, C), lambda b, c: (0, 0))],
        out_specs=(fwd, bwd),
        scratch_shapes=[pltpu.VMEM((2, C // LANE, LANE, LANE), F32)],
        compiler_params=_cparams(("arbitrary", "arbitrary")),
        name="rwkv_scan",
    )(r, k, v, kk, r, k, v, kk, iclr, iclr, logw, logw, k_a.reshape(1, C))


def _merge_kernel(of_ref, ob_ref, r_ref, k_ref, v_ref, g_ref, oa_ref, za_ref, zb_ref,
                  hs_ref, rk_ref, lnw_ref, lnb_ref, bga_ref, bgb_ref, wa_ref, wb_ref,
                  out_ref, yb_ref, oab_ref):
    @pl.when(pl.program_id(1) == 0)
    def _():
        hs = hs_ref[...]
        o = of_ref[...] + ob_ref[...]
        inv_n = 1.0 / RWKV_HEAD_DIM
        mean = _head_sums(o, hs) * inv_n
        cen = o - mean
        var = _head_sums(cen * cen, hs) * inv_n
        y = cen * lax.rsqrt(var + RWKV_GN_EPS) * lnw_ref[...] + lnb_ref[...]
        v = v_ref[...]
        bonus = _head_sums(r_ref[...] * k_ref[...] * rk_ref[...], hs) * v
        yb_ref[...] = ((y + bonus) * g_ref[...]).astype(BF16)
        oab_ref[...] = oa_ref[...].astype(BF16)

    ga = jax.nn.sigmoid(za_ref[...] + bga_ref[...])
    gb = jax.nn.sigmoid(zb_ref[...] + bgb_ref[...])
    out_ref[...] = ga * _dot(oab_ref[...], wa_ref[...]) + gb * _dot(yb_ref[...], wb_ref[...])


def _merge(o_f, o_b, r, k, v, g, oa, z, headsum, r_k, ln_w, ln_b, b_gate, wa, wb, D):
    T, C = r.shape
    tm, tn = 256, D
    nj = D // tn
    gcol = COL_GATE // tn
    tc = pl.BlockSpec((tm, C), lambda i, j: (i, 0))

    def vec(n):
        return pl.BlockSpec((1, n), lambda i, j: (0, 0))

    return pl.pallas_call(
        _merge_kernel,
        out_shape=jax.ShapeDtypeStruct((T, D), F32),
        grid=(T // tm, nj),
        in_specs=[tc, tc, tc, tc, tc, tc, tc,
                  pl.BlockSpec((tm, tn), lambda i, j: (i, gcol + j)),
                  pl.BlockSpec((tm, tn), lambda i, j: (i, gcol + nj + j)),
                  pl.BlockSpec((LANE, LANE), lambda i, j: (0, 0)), vec(C), vec(C), vec(C),
                  pl.BlockSpec((1, tn), lambda i, j: (0, j)),
                  pl.BlockSpec((1, tn), lambda i, j: (0, nj + j)),
                  pl.BlockSpec((C, tn), lambda i, j: (0, j)),
                  pl.BlockSpec((C, tn), lambda i, j: (0, j))],
        out_specs=pl.BlockSpec((tm, tn), lambda i, j: (i, j)),
        scratch_shapes=[pltpu.VMEM((tm, C), BF16), pltpu.VMEM((tm, C), BF16)],
        compiler_params=_cparams(("arbitrary", "arbitrary")),
        name="merge",
    )(o_f, o_b, r, k, v, g, oa, z, z, headsum, r_k.reshape(1, C), ln_w.reshape(1, C), ln_b.reshape(1, C),
      b_gate.reshape(1, 2 * D), b_gate.reshape(1, 2 * D), wa, wb)


def _outproj_kernel(m_ref, x_ref, mod_ref, nw_ref, w_ref, x1_ref, u2_ref):
    x1 = x_ref[...] + mod_ref[0, 2:3, :] * _dot(m_ref[...].astype(BF16), w_ref[...])
    x1_ref[...] = x1
    ms = jnp.mean(x1 * x1, axis=-1, keepdims=True)
    y = x1 * lax.rsqrt(ms + NORM_EPS) * nw_ref[...]
    u2_ref[...] = (y * (1.0 + mod_ref[0, 4:5, :]) + mod_ref[0, 3:4, :]).astype(BF16)


def _outproj(merged, x2, mod, norm_w, w_out, S):
    T, D = x2.shape
    tm = 256
    tpb = S // tm
    td = pl.BlockSpec((tm, D), lambda i: (i, 0))
    return pl.pallas_call(
        _outproj_kernel,
        out_shape=(jax.ShapeDtypeStruct((T, D), F32), jax.ShapeDtypeStruct((T, D), BF16)),
        grid=(T // tm,),
        in_specs=[td, td, pl.BlockSpec((1, 6, D), lambda i: (i // tpb, 0, 0)),
                  pl.BlockSpec((1, D), lambda i: (0, 0)), pl.BlockSpec((D, D), lambda i: (0, 0))],
        out_specs=(td, td),
        compiler_params=_cparams(("arbitrary",)),
        name="outproj",
    )(merged, x2, mod, norm_w.reshape(1, D), w_out)


PEER_NV = PEER_TOPK + 1
PEER_VPAD = 24


def _peer_topk_kernel(u_ref, wq_ref, k1_ref, k2_ref, e1_ref, th_ref, e2_ref, vals_ref):
    tm = u_ref.shape[0]
    half = PEER_DQ // 2
    q = _dot(u_ref[...], wq_ref[...])
    vals_ref[...] = jnp.full(vals_ref.shape, NEG, F32)
    for h in range(PEER_HEADS):
        q1 = q[:, h * PEER_DQ:h * PEER_DQ + half].astype(BF16)
        q2 = q[:, h * PEER_DQ + half:(h + 1) * PEER_DQ].astype(BF16)
        s1 = _dot_nt(k1_ref[...], q1)
        s2 = _dot_nt(k2_ref[...], q2)
        for side, s in enumerate((s1, s2)):
            for it in range(PEER_NV):
                m = s.max(axis=0, keepdims=True)
                vals_ref[side, it:it + 1, :] = m
                s = jnp.where(s >= m, NEG, s)
        v1 = vals_ref[0]
        v2 = vals_ref[1]
        cands = [v1[a:a + 1, :] + v2[0:8, :] for a in range(8)]
        cands.append(v1[0:1, :] + v2[8:PEER_VPAD, :])
        cands.append(v1[8:PEER_VPAD, :] + v2[0:1, :])
        cand = jnp.concatenate(cands, axis=0)
        crow = lax.broadcasted_iota(jnp.int32, cand.shape, 0)
        tops = []
        for it in range(PEER_NV):
            m = cand.max(axis=0, keepdims=True)
            tops.append(m)
            first = jnp.where(cand >= m, crow, cand.shape[0]).min(axis=0, keepdims=True)
            cand = jnp.where(crow == first, NEG, cand)
        zsum = jnp.zeros((1, tm), F32)
        for t in tops[:PEER_TOPK]:
            zsum = zsum + jnp.exp(t - tops[0])
        tau = 0.5 * (tops[PEER_TOPK - 1] + tops[PEER_TOPK])
        m1 = v1[0:1, :]
        m2 = v2[0:1, :]
        e1_ref[h] = jnp.exp(s1 - m1) * (0.5 / zsum)
        e2_ref[h] = jnp.exp(s2 - m2)
        th_ref[h] = jnp.exp(jnp.minimum(tau - m2 - s1, 80.0))


def _peer_topk(u2, w_q, k1, k2):
    T, D = u2.shape
    tm = 512
    out = jax.ShapeDtypeStruct((PEER_HEADS, PEER_N_KEYS, T), F32)
    ospec = pl.BlockSpec((PEER_HEADS, PEER_N_KEYS, tm), lambda i: (0, 0, i))
    return pl.pallas_call(
        _peer_topk_kernel,
        out_shape=(out, out, out),
        grid=(T // tm,),
        in_specs=[pl.BlockSpec((tm, D), lambda i: (i, 0)),
                  pl.BlockSpec(w_q.shape, lambda i: (0, 0)),
                  pl.BlockSpec(k1.shape, lambda i: (0, 0)),
                  pl.BlockSpec(k2.shape, lambda i: (0, 0))],
        out_specs=(ospec, ospec, ospec),
        scratch_shapes=[pltpu.VMEM((2, PEER_VPAD, tm), F32)],
        compiler_params=_cparams(("arbitrary",)),
        name="peer_topk",
    )(u2, w_q, k1, k2)


PEER_TE = 1024


def _peer_mix_kernel(u_ref, x1_ref, mod_ref, e1_ref, th_ref, e2_ref, eu_ref, ev_ref, o_ref,
                     h_ref, hid_ref, acc_ref):
    j = pl.program_id(1)
    tm = u_ref.shape[0]

    @pl.when(j == 0)
    def _():
        acc_ref[...] = jnp.zeros_like(acc_ref)

    h_ref[...] = _dot_nt(eu_ref[...], u_ref[...])
    for ii in range(PEER_TE // PEER_N_KEYS):
        rs = slice(ii * PEER_N_KEYS, (ii + 1) * PEER_N_KEYS)
        for tc in range(tm // LANE):
            cs = slice(tc * LANE, (tc + 1) * LANE)
            gate = jnp.zeros((PEER_N_KEYS, LANE), F32)
            for h in range(PEER_HEADS):
                e2 = e2_ref[h, :, cs]
                sel = jnp.where(e2 >= th_ref[h, ii:ii + 1, cs], e2, 0.0)
                gate = gate + sel * e1_ref[h, ii:ii + 1, cs]
            hb = h_ref[rs, cs]
            act = hb * (1.0 + lax.erf(hb * (2.0 ** -0.5)))
            hid_ref[rs, cs] = (act * gate).astype(BF16)
    acc_ref[...] += _dot_tn(hid_ref[...], ev_ref[...])

    @pl.when(j == pl.num_programs(1) - 1)
    def _():
        o_ref[...] = x1_ref[...] + mod_ref[0, 5:6, :] * acc_ref[...]


def _peer_mix(u2, x1, mod, e1, th, e2, exp_u, exp_v, S):
    T, D = u2.shape
    tm = 512
    te = PEER_TE
    n1 = te // PEER_N_KEYS
    tpb = S // tm
    ne = exp_u.shape[0] // te
    td = pl.BlockSpec((tm, D), lambda i, j: (i, 0))
    return pl.pallas_call(
        _peer_mix_kernel,
        out_shape=jax.ShapeDtypeStruct((T, D), F32),
        grid=(T // tm, ne),
        in_specs=[td, td, pl.BlockSpec((1, 6, D), lambda i, j: (i // tpb, 0, 0)),
                  pl.BlockSpec((PEER_HEADS, n1, tm), lambda i, j: (0, j, i)),
                  pl.BlockSpec((PEER_HEADS, n1, tm), lambda i, j: (0, j, i)),
                  pl.BlockSpec((PEER_HEADS, PEER_N_KEYS, tm), lambda i, j: (0, 0, i)),
                  pl.BlockSpec((te, D), lambda i, j: (j, 0)),
                  pl.BlockSpec((te, D), lambda i, j: (j, 0))],
        out_specs=td,
        scratch_shapes=[pltpu.VMEM((te, tm), F32), pltpu.VMEM((te, tm), BF16), pltpu.VMEM((tm, D), F32)],
        compiler_params=_cparams(("arbitrary", "arbitrary")),
        name="peer_mix",
    )(u2, x1, mod, e1, th, e2, exp_u, exp_v)


def _pad_cols(w, n):
    return jnp.pad(w, ((0, 0), (0, n - w.shape[1])))


def _pack_in_weights(w_in):
    na = 3 * NA_WIDTH
    rk = 3 * RWKV_WIDTH
    za, zb, zg = w_in[:, :na], w_in[:, na:na + rk + 640], w_in[:, na + rk + 640:]
    lora = zb[:, rk:]
    parts, start = [], 0
    for s in (DECAY_LORA, DECAY_LORA, AAA_LORA, AAA_LORA):
        parts.append(_pad_cols(lora[:, start:start + s], LANE))
        start += s
    parts.append(lora[:, start:start + GATE_LORA])
    lora_p = _pad_cols(jnp.concatenate(parts, axis=1), LORA_W)
    return jnp.concatenate([za, zb[:, :rk], zg, lora_p], axis=1).astype(BF16)


def _pack_mu_lora(mu):
    lora = mu[:, 3 * RWKV_WIDTH:]
    parts, start = [], 0
    for s in (DECAY_LORA, DECAY_LORA, AAA_LORA, AAA_LORA):
        parts.append(_pad_cols(lora[:, start:start + s], LANE))
        start += s
    parts.append(lora[:, start:start + GATE_LORA])
    return _pad_cols(jnp.concatenate(parts, axis=1), LORA_W)


def _block(x, c, ada_w, ada_b, norm1_w, norm2_w, w_in, b_gate, na_q_norm_w, na_k_norm_w, na_rpb,
           rwkv_mu, rwkv_w0, rwkv_w2, rwkv_a0, rwkv_a2, rwkv_g2, rwkv_k_k, rwkv_k_a, rwkv_r_k,
           rwkv_ln_w, rwkv_ln_b, w_branch_a, w_branch_b, w_out, peer_w_q, peer_sub_k1, peer_sub_k2,
           peer_u, peer_v):
    B, S, D = x.shape
    T = B * S
    x2 = x.reshape(T, D)
    mod = _adaln(c, ada_w, ada_b)

    z = _inproj(x2, mod, norm1_w, _pack_in_weights(w_in), S)

    o_a = _na(z, na_rpb, na_q_norm_w, na_k_norm_w, B, S)

    head = jnp.arange(LANE) // RWKV_HEAD_DIM
    headsum = (head[:, None] == head[None, :]).astype(BF16)
    w2p = jnp.pad(rwkv_w2, ((0, 0), (0, LANE - DECAY_LORA), (0, 0))).astype(BF16)
    a2p = jnp.pad(rwkv_a2, ((0, 0), (0, LANE - AAA_LORA), (0, 0))).astype(BF16)
    r, k, v, kk, iclr, logw, g = _rwkv_prep(
        z, rwkv_mu[:, :3 * RWKV_WIDTH], _pack_mu_lora(rwkv_mu), rwkv_w0, w2p, rwkv_a0, a2p,
        rwkv_g2.astype(BF16), rwkv_k_k, headsum, S)
    o_f, o_b = _rwkv_scan(r, k, v, kk, iclr, logw, rwkv_k_a, B, S)

    merged = _merge(o_f, o_b, r, k, v, g, o_a, z, headsum, rwkv_r_k.reshape(-1), rwkv_ln_w, rwkv_ln_b,
                    b_gate, w_branch_a.astype(BF16), w_branch_b.astype(BF16), D)
    x1, u2 = _outproj(merged, x2, mod, norm2_w, w_out.astype(BF16), S)

    e1, th, e2 = _peer_topk(u2, peer_w_q.astype(BF16), peer_sub_k1.astype(BF16), peer_sub_k2.astype(BF16))
    out = _peer_mix(u2, x1, mod, e1, th, e2, peer_u.astype(BF16), peer_v.astype(BF16), S)
    return out.reshape(B, S, D)


def kernel(x, c, ada_w, ada_b, norm1_w, norm2_w, w_in, b_gate, na_q_norm_w, na_k_norm_w, na_rpb, rwkv_mu, rwkv_w0, rwkv_w2, rwkv_a0, rwkv_a2, rwkv_g2, rwkv_k_k, rwkv_k_a, rwkv_r_k, rwkv_ln_w, rwkv_ln_b, w_branch_a, w_branch_b, w_out, peer_w_q, peer_sub_k1, peer_sub_k2, peer_u, peer_v):
    h = x
    for layer in range(ada_w.shape[0]):
        h = _block(h, c, ada_w[layer], ada_b[layer], norm1_w[layer], norm2_w[layer], w_in[layer],
                   b_gate[layer], na_q_norm_w[layer], na_k_norm_w[layer], na_rpb[layer],
                   rwkv_mu[layer], rwkv_w0[layer], rwkv_w2[layer], rwkv_a0[layer], rwkv_a2[layer],
                   rwkv_g2[layer], rwkv_k_k[layer], rwkv_k_a[layer], rwkv_r_k[layer],
                   rwkv_ln_w[layer], rwkv_ln_b[layer], w_branch_a[layer], w_branch_b[layer],
                   w_out[layer], peer_w_q[layer], peer_sub_k1[layer], peer_sub_k2[layer],
                   peer_u[layer], peer_v[layer])
    return h
```
